```python
import jax
import jax.numpy as jnp
from jax import lax
import numpy as np


D_MODEL = 2048
BATCH = 8
SEQ = 2048
DEPTH = 4
DEC_BATCH = 2
DEC_SEQ = 4096
PAST_LEN = 128

N_META = 16
BLOCK = 128
META_START = BLOCK - N_META
REAL_START = BLOCK
EPS = 1e-6
M_HEADS = 8
M_DQK = 128
M_DV = 256
A_HEADS = 16
A_KV_HEADS = 4
A_DH = 128
WINDOW = 128
ROPE_THETA = 10000.0
P_HEADS = 8
P_NKEYS = 128
P_NEXP = P_NKEYS * P_NKEYS
P_DQ = 256
P_TOPK = 16
P_TOK_BLOCK = 128
SPLIT_SIZES = (M_HEADS * M_DQK, M_HEADS * M_DQK, M_HEADS * M_DV, M_HEADS * M_DV, 4 * M_HEADS,
               A_HEADS * A_DH, A_KV_HEADS * A_DH, A_KV_HEADS * A_DH, 2 * D_MODEL)
SPLIT_POINTS = tuple(int(s) for s in np.cumsum(SPLIT_SIZES)[:-1])
D_IN = sum(SPLIT_SIZES)
M_V = M_HEADS * M_DV
A_Q = A_HEADS * A_DH

kernel_name = 'hybrid_mlstm_swa_peer_encoder'

F32 = jnp.float32


def rmsnorm(x, g):
    xf = x.astype(F32)
    y = xf * lax.rsqrt(jnp.mean(xf * xf, axis=-1, keepdims=True) + EPS)
    return (y * g.astype(F32)).astype(x.dtype)


def rope(x, pos):
    half = x.shape[-1] // 2
    inv = ROPE_THETA ** (-jnp.arange(half, dtype=F32) / half)
    ang = pos[:, None] * inv[None, :]
    cos = jnp.cos(ang)[None, :, None, :]
    sin = jnp.sin(ang)[None, :, None, :]
    xf = x.astype(F32)
    x1, x2 = xf[..., :half], xf[..., half:]
    return jnp.concatenate([x1 * cos - x2 * sin, x2 * cos + x1 * sin], axis=-1)


def mlstm_chunkwise(q, k, v, li, lf):
    B, Lp, H, dk = q.shape
    dv = v.shape[-1]
    nc = Lp // BLOCK
    tri = jnp.tril(jnp.ones((BLOCK, BLOCK), bool))[None, :, :, None]

    def chunks(a):
        return jnp.moveaxis(a.reshape((B, nc, BLOCK) + a.shape[2:]), 1, 0)

    def step(carry, inp):
        C, n, m = carry
        qc, kc, vc, lic, lfc = inp
        b = jnp.cumsum(lfc, axis=1)
        Dm = b[:, :, None, :] - b[:, None, :, :] + lic[:, None, :, :]
        Dm = jnp.where(tri, Dm, -jnp.inf)
        inter = b + m[:, None, :]
        m_t = jnp.maximum(inter, jnp.max(Dm, axis=2))
        P = jnp.exp(Dm - m_t[:, :, None, :]) * jnp.einsum('bthd,bshd->btsh', qc, kc)
        w_inter = jnp.exp(inter - m_t)
        num = jnp.einsum('btsh,bshv->bthv', P, vc) + w_inter[..., None] * jnp.einsum('bhvd,bthd->bthv', C, qc)
        den = jnp.sum(P, axis=2) + w_inter * jnp.einsum('bhd,bthd->bth', n, qc)
        h = num / jnp.maximum(jnp.abs(den), jnp.exp(-m_t))[..., None]
        bL = b[:, -1]
        g = bL[:, None, :] - b + lic
        m_new = jnp.maximum(bL + m, jnp.max(g, axis=1))
        wk = jnp.exp(g - m_new[:, None, :])
        decay = jnp.exp(bL + m - m_new)
        C_new = decay[..., None, None] * C + jnp.einsum('blh,blhv,blhd->bhvd', wk, vc, kc)
        n_new = decay[..., None] * n + jnp.einsum('blh,blhd->bhd', wk, kc)
        return (C_new, n_new, m_new), h

    init = (jnp.zeros((B, H, dv, dk), F32), jnp.zeros((B, H, dk), F32), jnp.zeros((B, H), F32))
    _, hs = lax.scan(step, init, (chunks(q), chunks(k), chunks(v), chunks(li), chunks(lf)))
    return jnp.moveaxis(hs, 0, 1).reshape(B, Lp, H, dv)


def mlstm_bidirectional(q, k, v, gates, valid):
    vmask = valid[None, :, None]
    i_f, f_f, i_b, f_b = jnp.split(gates, 4, axis=-1)

    def prep(i_pre, f_pre):
        return jnp.where(vmask, i_pre, -jnp.inf), jnp.where(vmask, jax.nn.log_sigmoid(f_pre), 0.0)

    li, lf = prep(i_f, f_f)
    h_fwd = mlstm_chunkwise(q, k, v, li, lf)
    li, lf = prep(i_b, f_b)
    rev = lambda a: jnp.flip(a, axis=1)
    h_bwd = rev(mlstm_chunkwise(rev(q), rev(k), rev(v), rev(li), rev(lf)))
    return h_fwd + h_bwd


def banded_attention(q, k, v, sink):
    B, Lp = q.shape[:2]
    nb = Lp // BLOCK
    G = A_HEADS // A_KV_HEADS
    qb = q.reshape(B, nb, BLOCK, A_KV_HEADS, G, A_DH)

    def band(a):
        ab = a.reshape(B, nb, BLOCK, A_KV_HEADS, A_DH)
        ap = jnp.pad(ab, ((0, 0), (1, 1), (0, 0), (0, 0), (0, 0)))
        return jnp.concatenate([ap[:, :-2], ap[:, 1:-1], ap[:, 2:]], axis=2)

    kw, vw = band(k), band(v)
    km, vm = k[:, META_START:REAL_START], v[:, META_START:REAL_START]
    s_win = jnp.einsum('bnqhgd,bnshd->bnhgqs', qb, kw).astype(F32)
    s_meta = jnp.einsum('bnqhgd,bmhd->bnhgqm', qb, km).astype(F32)
    qi = jnp.arange(nb)[:, None, None] * BLOCK + jnp.arange(BLOCK)[None, :, None]
    kj = (jnp.arange(nb)[:, None, None] - 1) * BLOCK + jnp.arange(3 * BLOCK)[None, None, :]
    mask = (kj >= REAL_START) & (kj < Lp) & (jnp.abs(qi - kj) <= WINDOW)
    s_win = jnp.where(mask[None, :, None, None], s_win, -jnp.inf)
    s_sink = jnp.broadcast_to(sink.astype(F32).reshape(1, 1, A_KV_HEADS, G, 1, 1), s_win.shape[:-1] + (1,))
    p = jax.nn.softmax(jnp.concatenate([s_win, s_meta, s_sink], axis=-1), axis=-1)
    o = (jnp.einsum('bnhgqs,bnshd->bnqhgd', p[..., :3 * BLOCK], vw)
         + jnp.einsum('bnhgqm,bmhd->bnqhgd', p[..., 3 * BLOCK:3 * BLOCK + N_META], vm))
    return o.reshape(B, Lp, A_HEADS * A_DH)


def peer(x, w_pq, sub_k1, sub_k2, u_tab, v_tab):
    B, Lp, D = x.shape
    T = B * Lp
    t = x.reshape(T, D)
    q = (t @ w_pq).astype(F32).reshape(T, P_HEADS, 2, P_DQ // 2)
    s1 = jnp.einsum('thd,hkd->thk', q[:, :, 0], sub_k1.astype(F32))
    s2 = jnp.einsum('thd,hkd->thk', q[:, :, 1], sub_k2.astype(F32))
    v1, i1 = lax.top_k(s1, P_TOPK)
    v2, i2 = lax.top_k(s2, P_TOPK)
    cand = (v1[..., :, None] + v2[..., None, :]).reshape(T, P_HEADS, P_TOPK * P_TOPK)
    sv, si = lax.top_k(cand, P_TOPK)
    idx = (jnp.take_along_axis(i1, si // P_TOPK, axis=-1) * P_NKEYS
           + jnp.take_along_axis(i2, si % P_TOPK, axis=-1))
    gate = jax.nn.softmax(sv, axis=-1)
    nblk = T // P_TOK_BLOCK
    HK = P_HEADS * P_TOPK

    def token_block(args):
        tb, eb, gb = args
        ub = jnp.take(u_tab, eb, axis=0)
        act = jax.nn.gelu(jnp.einsum('td,tkd->tk', tb, ub).astype(F32), approximate=False)
        return jnp.einsum('tk,tkd->td', (gb * act).astype(tb.dtype), jnp.take(v_tab, eb, axis=0))

    y = lax.map(token_block, (t.reshape(nblk, P_TOK_BLOCK, D), idx.reshape(nblk, P_TOK_BLOCK, HK),
                              gate.reshape(nblk, P_TOK_BLOCK, HK)))
    return y.reshape(B, Lp, D).astype(x.dtype)


def encoder_layer(h, valid, pos, g_mix, w_in, b_mgate, g_mhead, g_q, g_k, attn_sink,
                  w_branch_m, w_branch_a, b_merge, w_out, g_ffn, w_pq, sub_k1, sub_k2, u_tab, v_tab):
    B, Lp, _ = h.shape
    dt = h.dtype
    xn = rmsnorm(h, g_mix)
    mq, mk, mv, mo, mg, aq, ak, av, gm = jnp.split(xn @ w_in, SPLIT_POINTS, axis=-1)
    q_m = mq.reshape(B, Lp, M_HEADS, M_DQK).astype(F32)
    k_m = mk.reshape(B, Lp, M_HEADS, M_DQK).astype(F32) * (M_DQK ** -0.5)
    v_m = mv.reshape(B, Lp, M_HEADS, M_DV).astype(F32)
    h_m = mlstm_bidirectional(q_m, k_m, v_m, mg.astype(F32) + b_mgate.astype(F32), valid)
    h_m = rmsnorm(h_m, g_mhead) * jax.nn.sigmoid(mo.astype(F32)).reshape(B, Lp, M_HEADS, M_DV)
    y_m = h_m.reshape(B, Lp, M_V).astype(dt) @ w_branch_m
    q_a = rope(rmsnorm(aq.reshape(B, Lp, A_HEADS, A_DH), g_q), pos) * (A_DH ** -0.5)
    k_a = rope(rmsnorm(ak.reshape(B, Lp, A_KV_HEADS, A_DH), g_k), pos)
    v_a = av.reshape(B, Lp, A_KV_HEADS, A_DH)
    y_a = banded_attention(q_a, k_a, v_a, attn_sink).astype(dt) @ w_branch_a
    gate_m, gate_a = jnp.split(jax.nn.sigmoid((gm + b_merge).astype(F32)), 2, axis=-1)
    h = h + (gate_m * y_m + gate_a * y_a).astype(dt) @ w_out
    h = h + peer(rmsnorm(h, g_ffn), w_pq, sub_k1, sub_k2, u_tab, v_tab)
    return h


def encoder_trunk(x, meta_tokens, layer_weights):
    B, L, D = x.shape
    Lp = REAL_START + L
    meta = jnp.broadcast_to(meta_tokens.astype(x.dtype)[None], (B, N_META, D))
    h = jnp.concatenate([jnp.zeros((B, META_START, D), x.dtype), meta, x], axis=1)
    idx = jnp.arange(Lp)
    valid = idx >= META_START
    pos = (idx - META_START).astype(F32)
    for l in range(DEPTH):
        h = encoder_layer(h, valid, pos, *[w[l] for w in layer_weights])
    return h[:, REAL_START:]


def setup_inputs(seed: int = 0) -> dict:
    key = jax.random.key(seed)
    ks = jax.random.split(key, 24)
    nrm = lambda k, shape, scale: jax.random.normal(k, shape, F32) * scale
    f_bias = jnp.linspace(3.0, 6.0, M_HEADS, dtype=F32)
    zb = jnp.zeros((M_HEADS,), F32)
    gate_bias = jnp.concatenate([zb, f_bias, zb, f_bias])
    return {
        'x_prompt': nrm(ks[0], (BATCH, SEQ, D_MODEL), 1.0),
        'x_sample': nrm(ks[1], (DEC_BATCH, DEC_SEQ, D_MODEL), 1.0),
        'meta_tokens': nrm(ks[2], (N_META, D_MODEL), 1.0),
        'g_mix': 1.0 + nrm(ks[3], (DEPTH, D_MODEL), 0.02),
        'w_in': nrm(ks[4], (DEPTH, D_MODEL, D_IN), D_MODEL ** -0.5),
        'b_mgate': gate_bias[None, :] + nrm(ks[5], (DEPTH, 4 * M_HEADS), 0.1),
        'g_mhead': 1.0 + nrm(ks[6], (DEPTH, M_HEADS, M_DV), 0.02),
        'g_q': 1.0 + nrm(ks[7], (DEPTH, A_DH), 0.02),
        'g_k': 1.0 + nrm(ks[8], (DEPTH, A_DH), 0.02),
        'attn_sink': nrm(ks[9], (DEPTH, A_HEADS), 0.5),
        'w_branch_m': nrm(ks[10], (DEPTH, M_V, D_MODEL), M_V ** -0.5),
        'w_branch_a': nrm(ks[11], (DEPTH, A_Q, D_MODEL), A_Q ** -0.5),
        'b_merge': nrm(ks[12], (DEPTH, 2 * D_MODEL), 0.1),
        'w_out': nrm(ks[13], (DEPTH, D_MODEL, D_MODEL), D_MODEL ** -0.5),
        'g_ffn': 1.0 + nrm(ks[14], (DEPTH, D_MODEL), 0.02),
        'w_pq': nrm(ks[15], (DEPTH, D_MODEL, P_HEADS * P_DQ), D_MODEL ** -0.5),
        'sub_k1': nrm(ks[16], (DEPTH, P_HEADS, P_NKEYS, P_DQ // 2), (P_DQ // 2) ** -0.5),
        'sub_k2': nrm(ks[17], (DEPTH, P_HEADS, P_NKEYS, P_DQ // 2), (P_DQ // 2) ** -0.5),
        'u_tab': nrm(ks[18], (DEPTH, P_NEXP, D_MODEL), D_MODEL ** -0.5),
        'v_tab': nrm(ks[19], (DEPTH, P_NEXP, D_MODEL), (P_HEADS * P_TOPK) ** -0.5),
    }


def reference(x_prompt, x_sample, meta_tokens, g_mix, w_in, b_mgate, g_mhead, g_q, g_k, attn_sink,
              w_branch_m, w_branch_a, b_merge, w_out, g_ffn, w_pq, sub_k1, sub_k2, u_tab, v_tab):
    layer_weights = (g_mix, w_in, b_mgate, g_mhead, g_q, g_k, attn_sink, w_branch_m, w_branch_a,
                     b_merge, w_out, g_ffn, w_pq, sub_k1, sub_k2, u_tab, v_tab)
    y_prompt = encoder_trunk(x_prompt, meta_tokens, layer_weights)
    y_sample = encoder_trunk(x_sample, meta_tokens, layer_weights)
    return (y_prompt, y_sample)
```

```python
import functools

import jax
import jax.numpy as jnp
from jax import lax
from jax.experimental import pallas as pl
from jax.experimental.pallas import tpu as pltpu

F32 = jnp.float32
BF16 = jnp.bfloat16

D_MODEL = 2048
DEPTH = 4
N_META = 16
BLOCK = 128
META_START = BLOCK - N_META
EPS = 1e-6
M_HEADS = 8
M_DQK = 128
M_DV = 256
A_HEADS = 16
A_KV_HEADS = 4
A_GROUP = A_HEADS // A_KV_HEADS
A_DH = 128
WINDOW = 128
ROPE_THETA = 10000.0
P_HEADS = 8
P_NKEYS = 128
P_NEXP = P_NKEYS * P_NKEYS
P_DQ = 256
P_TOPK = 16

M_QK = M_HEADS * M_DQK
M_V = M_HEADS * M_DV
A_Q = A_HEADS * A_DH
A_KV = A_KV_HEADS * A_DH
OFF_MQ = 0
OFF_MK = OFF_MQ + M_QK
OFF_MV = OFF_MK + M_QK
OFF_MO = OFF_MV + M_V
OFF_AQ = OFF_MO + M_V
OFF_AK = OFF_AQ + A_Q
OFF_AV = OFF_AK + A_KV
OFF_GM = OFF_AV + A_KV
N_MAIN = OFF_GM + 2 * D_MODEL
N_GATE = 4 * M_HEADS
LANES = 128
STATE_W = M_DV + LANES
VMEM_LIMIT = 56 * 1024 * 1024


def _pick_tile(n, cap, mult):
    best = None
    for t in range(mult, min(n, cap) + 1, mult):
        if n % t == 0:
            best = t
    assert best is not None, (n, cap, mult)
    return best


def _params(sem):
    return pltpu.CompilerParams(dimension_semantics=sem, vmem_limit_bytes=VMEM_LIMIT)


def _norm_proj_kernel(h_ref, g_ref, w_ref, o_ref, xn_ref):
    @pl.when(pl.program_id(1) == 0)
    def _():
        x = h_ref[...]
        ms = jnp.mean(x * x, axis=-1, keepdims=True)
        xn_ref[...] = (x * lax.rsqrt(ms + EPS) * g_ref[...]).astype(BF16)

    o_ref[...] = jnp.dot(xn_ref[...], w_ref[...], preferred_element_type=F32)


def _norm_proj(h, g, w, tn):
    T, D = h.shape
    N = w.shape[1]
    tm = _pick_tile(T, 1024, 256)
    return pl.pallas_call(
        _norm_proj_kernel,
        grid=(T // tm, N // tn),
        in_specs=[pl.BlockSpec((tm, D), lambda i, j: (i, 0)),
                  pl.BlockSpec((1, D), lambda i, j: (0, 0)),
                  pl.BlockSpec((D, tn), lambda i, j: (0, j))],
        out_specs=pl.BlockSpec((tm, tn), lambda i, j: (i, j)),
        out_shape=jax.ShapeDtypeStruct((T, N), F32),
        scratch_shapes=[pltpu.VMEM((tm, D), BF16)],
        compiler_params=_params(("parallel", "arbitrary")),
        name="norm_proj",
    )(h, g, w)


def _mlstm_direction(q_ref, k_ref, v_ref, g_ref, bias_ref, o_ref, s_ref, m_ref, tok0, fwd):
    T = BLOCK
    g = g_ref[...] + bias_ref[...]
    rows = lax.broadcasted_iota(jnp.int32, (T, LANES), 0)
    valid = (tok0 + rows) >= META_START
    logsig = jnp.minimum(g, 0.0) - jnp.log1p(jnp.exp(-jnp.abs(g)))
    lf = jnp.where(valid, logsig, 0.0)
    li = jnp.where(valid, g, -jnp.inf)
    r = lax.broadcasted_iota(jnp.int32, (T, T), 0)
    c = lax.broadcasted_iota(jnp.int32, (T, T), 1)
    tri = (c <= r) if fwd else (c >= r)
    bcum = jnp.dot(tri.astype(F32), lf, preferred_element_type=F32,
                   precision=lax.Precision.HIGHEST)
    bcum_t = bcum.T
    li_t = li.T
    i_off = 0 if fwd else 2 * M_HEADS
    f_off = i_off + M_HEADS
    last = T - 1 if fwd else 0
    ones = jnp.ones((T, LANES), F32)
    for h in range(M_HEADS):
        b_col = bcum[:, f_off + h:f_off + h + 1]
        b_row = bcum_t[f_off + h:f_off + h + 1, :]
        li_col = li[:, i_off + h:i_off + h + 1]
        li_row = li_t[i_off + h:i_off + h + 1, :]
        b_last = bcum[last:last + 1, f_off + h:f_off + h + 1]
        m_old = m_ref[h:h + 1, 0:1]
        qb = q_ref[:, h * M_DQK:(h + 1) * M_DQK].astype(BF16)
        kb = (k_ref[:, h * M_DQK:(h + 1) * M_DQK] * (M_DQK ** -0.5)).astype(BF16)
        v_ext = jnp.concatenate([v_ref[:, h * M_DV:(h + 1) * M_DV], ones], axis=1)
        state = s_ref[h]

        dm = jnp.where(tri, b_col - b_row + li_row, -jnp.inf)
        inter = b_col + m_old
        m_t = jnp.maximum(inter, jnp.max(dm, axis=1, keepdims=True))
        s_qk = lax.dot_general(qb, kb, (((1,), (1,)), ((), ())), preferred_element_type=F32)
        p = (jnp.exp(dm - m_t) * s_qk).astype(BF16)
        w_inter = jnp.exp(inter - m_t)
        nd = (jnp.dot(p, v_ext.astype(BF16), preferred_element_type=F32)
              + w_inter * jnp.dot(qb, state.astype(BF16), preferred_element_type=F32))
        den = jnp.maximum(jnp.abs(nd[:, M_DV:]), jnp.exp(-m_t))
        inv = 1.0 / den
        o_ref[:, h * M_DV:(h + 1) * M_DV] = nd[:, :M_DV] * jnp.concatenate([inv, inv], axis=1)

        gk = b_last - b_col + li_col
        m_new = jnp.maximum(b_last + m_old, jnp.max(gk, axis=0, keepdims=True))
        wk = jnp.exp(gk - m_new)
        decay = jnp.exp(b_last + m_old - m_new)
        upd = lax.dot_general(kb, (wk * v_ext).astype(BF16), (((0,), (0,)), ((), ())),
                              preferred_element_type=F32)
        s_ref[h] = decay * state + upd
        m_ref[h:h + 1, :] = jnp.broadcast_to(m_new, (1, LANES))


def _mlstm_kernel(qf, kf, vf, gf, qb, kb, vb, gb, bias, of, ob, sf, sb, mf, mb):
    c = pl.program_id(1)
    nc = pl.num_programs(1)

    @pl.when(c == 0)
    def _():
        sf[...] = jnp.zeros_like(sf)
        sb[...] = jnp.zeros_like(sb)
        mf[...] = jnp.zeros_like(mf)
        mb[...] = jnp.zeros_like(mb)

    _mlstm_direction(qf, kf, vf, gf, bias, of, sf, mf, c * BLOCK, True)
    _mlstm_direction(qb, kb, vb, gb, bias, ob, sb, mb, (nc - 1 - c) * BLOCK, False)


def _mlstm(proj, gates, bias, B, nc):
    T = proj.shape[0]

    def fwd_map(col):
        return lambda b, c: (b * nc + c, col)

    def bwd_map(col):
        return lambda b, c: (b * nc + nc - 1 - c, col)

    def specs(mk):
        return [pl.BlockSpec((BLOCK, M_QK), mk(OFF_MQ // M_QK)),
                pl.BlockSpec((BLOCK, M_QK), mk(OFF_MK // M_QK)),
                pl.BlockSpec((BLOCK, M_V), mk(OFF_MV // M_V))]

    gate_f = pl.BlockSpec((BLOCK, LANES), fwd_map(0))
    gate_b = pl.BlockSpec((BLOCK, LANES), bwd_map(0))
    out = jax.ShapeDtypeStruct((T, M_V), F32)
    return pl.pallas_call(
        _mlstm_kernel,
        grid=(B, nc),
        in_specs=(specs(fwd_map) + [gate_f] + specs(bwd_map) + [gate_b]
                  + [pl.BlockSpec((1, LANES), lambda b, c: (0, 0))]),
        out_specs=[pl.BlockSpec((BLOCK, M_V), fwd_map(0)), pl.BlockSpec((BLOCK, M_V), bwd_map(0))],
        out_shape=[out, out],
        scratch_shapes=[pltpu.VMEM((M_HEADS, M_DQK, STATE_W), F32),
                        pltpu.VMEM((M_HEADS, M_DQK, STATE_W), F32),
                        pltpu.VMEM((M_HEADS, LANES), F32),
                        pltpu.VMEM((M_HEADS, LANES), F32)],
        compiler_params=_params(("parallel", "arbitrary")),
        name="mlstm",
    )(proj, proj, proj, gates, proj, proj, proj, gates, bias)


def _qk_prep_kernel(q_ref, k_ref, v_ref, cos_ref, sin_ref, gq_ref, gk_ref, qo_ref, ko_ref, vo_ref):
    cos = cos_ref[...]
    sin = sin_ref[...]

    def norm_rope(x, g):
        xn = x * lax.rsqrt(jnp.mean(x * x, axis=-1, keepdims=True) + EPS) * g
        return xn * cos + pltpu.roll(xn, A_DH // 2, axis=1) * sin

    for h in range(A_HEADS):
        sl = slice(h * A_DH, (h + 1) * A_DH)
        qo_ref[:, sl] = (norm_rope(q_ref[:, sl], gq_ref[...]) * (A_DH ** -0.5)).astype(BF16)
    for h in range(A_KV_HEADS):
        sl = slice(h * A_DH, (h + 1) * A_DH)
        ko_ref[:, sl] = norm_rope(k_ref[:, sl], gk_ref[...]).astype(BF16)
    vo_ref[...] = v_ref[...].astype(BF16)


def _qk_prep(proj, cos, sin, g_q, g_k):
    T = proj.shape[0]
    tm = _pick_tile(T, 512, 128)
    row = lambda col: (lambda i: (i, col))
    const = lambda i: (0, 0)
    return pl.pallas_call(
        _qk_prep_kernel,
        grid=(T // tm,),
        in_specs=[pl.BlockSpec((tm, A_Q), row(OFF_AQ // A_Q)),
                  pl.BlockSpec((tm, A_KV), row(OFF_AK // A_KV)),
                  pl.BlockSpec((tm, A_KV), row(OFF_AV // A_KV)),
                  pl.BlockSpec((tm, A_DH), row(0)),
                  pl.BlockSpec((tm, A_DH), row(0)),
                  pl.BlockSpec((1, A_DH), const),
                  pl.BlockSpec((1, A_DH), const)],
        out_specs=[pl.BlockSpec((tm, A_Q), row(0)),
                   pl.BlockSpec((tm, A_KV), row(0)),
                   pl.BlockSpec((tm, A_KV), row(0))],
        out_shape=[jax.ShapeDtypeStruct((T, A_Q), BF16),
                   jax.ShapeDtypeStruct((T, A_KV), BF16),
                   jax.ShapeDtypeStruct((T, A_KV), BF16)],
        compiler_params=_params(("parallel",)),
        name="qk_prep",
    )(proj, proj, proj, cos, sin, g_q, g_k)


def _attn_kernel(sink_ref, q_ref, kp, kc, kn, km, vp, vc, vn, vm, o_ref):
    n = pl.program_id(1)
    nb = pl.num_programs(1)
    R = A_GROUP * BLOCK
    W = 4 * BLOCK
    row = lax.broadcasted_iota(jnp.int32, (R, W), 0)
    col = lax.broadcasted_iota(jnp.int32, (R, W), 1)
    t = row % BLOCK
    seg = col // BLOCK
    off = col % BLOCK
    dist = (1 - seg) * BLOCK + (t - off)
    blk = n - 1 + seg
    win_ok = (seg < 3) & (jnp.abs(dist) <= WINDOW) & (blk >= 1) & (blk <= nb - 1)
    meta_ok = (seg == 3) & (off >= META_START)
    mask = win_ok | meta_ok
    head_row = lax.broadcasted_iota(jnp.int32, (R, 1), 0) // BLOCK
    for j in range(A_KV_HEADS):
        sl = slice(j * A_DH, (j + 1) * A_DH)
        q4 = jnp.concatenate([q_ref[:, (j * A_GROUP + g) * A_DH:(j * A_GROUP + g + 1) * A_DH]
                              for g in range(A_GROUP)], axis=0)
        kcat = jnp.concatenate([kp[:, sl], kc[:, sl], kn[:, sl], km[:, sl]], axis=0)
        vcat = jnp.concatenate([vp[:, sl], vc[:, sl], vn[:, sl], vm[:, sl]], axis=0)
        s = lax.dot_general(q4, kcat, (((1,), (1,)), ((), ())), preferred_element_type=F32)
        s = jnp.where(mask, s, -jnp.inf)
        sink = jnp.zeros((R, 1), F32)
        for g in range(A_GROUP):
            sink = jnp.where(head_row == g, sink_ref[j * A_GROUP + g], sink)
        m = jnp.maximum(jnp.max(s, axis=1, keepdims=True), sink)
        p = jnp.exp(s - m)
        denom = jnp.sum(p, axis=1, keepdims=True) + jnp.exp(sink - m)
        o = jnp.dot(p.astype(BF16), vcat, preferred_element_type=F32) / denom
        for g in range(A_GROUP):
            hh = j * A_GROUP + g
            o_ref[:, hh * A_DH:(hh + 1) * A_DH] = o[g * BLOCK:(g + 1) * BLOCK].astype(BF16)


def _attention(q, k, v, sink, B, nb):
    T = q.shape[0]
    cur = lambda b, n, s: (b * nb + n, 0)
    prev = lambda b, n, s: (b * nb + jnp.maximum(n - 1, 0), 0)
    nxt = lambda b, n, s: (b * nb + jnp.minimum(n + 1, nb - 1), 0)
    meta = lambda b, n, s: (b * nb, 0)
    kv = lambda m: pl.BlockSpec((BLOCK, A_KV), m)
    grid_spec = pltpu.PrefetchScalarGridSpec(
        num_scalar_prefetch=1,
        grid=(B, nb),
        in_specs=[pl.BlockSpec((BLOCK, A_Q), cur),
                  kv(prev), kv(cur), kv(nxt), kv(meta), kv(prev), kv(cur), kv(nxt), kv(meta)],
        out_specs=pl.BlockSpec((BLOCK, A_Q), cur))
    return pl.pallas_call(
        _attn_kernel,
        grid_spec=grid_spec,
        out_shape=jax.ShapeDtypeStruct((T, A_Q), BF16),
        compiler_params=_params(("parallel", "arbitrary")),
        name="band_attn",
    )(sink, q, k, k, k, k, v, v, v, v)


def _merge_kernel(hf_ref, hb_ref, mo_ref, gh_ref, oa_ref, gmm_ref, gma_ref, bm_ref, ba_ref,
                  wm_ref, wa_ref, z_ref, hm_ref):
    @pl.when(pl.program_id(1) == 0)
    def _():
        for h in range(M_HEADS):
            sl = slice(h * M_DV, (h + 1) * M_DV)
            x = hf_ref[:, sl] + hb_ref[:, sl]
            xn = x * lax.rsqrt(jnp.mean(x * x, axis=-1, keepdims=True) + EPS) * gh_ref[:, sl]
            hm_ref[:, sl] = (xn * jax.nn.sigmoid(mo_ref[:, sl])).astype(BF16)

    y_m = jnp.dot(hm_ref[...], wm_ref[...], preferred_element_type=F32)
    y_a = jnp.dot(oa_ref[...], wa_ref[...], preferred_element_type=F32)
    gate_m = jax.nn.sigmoid(gmm_ref[...] + bm_ref[...])
    gate_a = jax.nn.sigmoid(gma_ref[...] + ba_ref[...])
    z_ref[...] = (gate_m * y_m + gate_a * y_a).astype(BF16)


def _merge(hf, hb, proj, g_mhead, oa, b_merge, wm, wa):
    T = hf.shape[0]
    tm = _pick_tile(T, 256, 128)
    tn = 512
    nj = D_MODEL // tn
    full = lambda col: (lambda i, j: (i, col))
    return pl.pallas_call(
        _merge_kernel,
        grid=(T // tm, nj),
        in_specs=[pl.BlockSpec((tm, M_V), full(0)),
                  pl.BlockSpec((tm, M_V), full(0)),
                  pl.BlockSpec((tm, M_V), full(OFF_MO // M_V)),
                  pl.BlockSpec((1, M_V), lambda i, j: (0, 0)),
                  pl.BlockSpec((tm, A_Q), full(0)),
                  pl.BlockSpec((tm, tn), lambda i, j: (i, OFF_GM // tn + j)),
                  pl.BlockSpec((tm, tn), lambda i, j: (i, OFF_GM // tn + nj + j)),
                  pl.BlockSpec((1, tn), lambda i, j: (0, j)),
                  pl.BlockSpec((1, tn), lambda i, j: (0, nj + j)),
                  pl.BlockSpec((M_V, tn), lambda i, j: (0, j)),
                  pl.BlockSpec((A_Q, tn), lambda i, j: (0, j))],
        out_specs=pl.BlockSpec((tm, tn), lambda i, j: (i, j)),
        out_shape=jax.ShapeDtypeStruct((T, D_MODEL), BF16),
        scratch_shapes=[pltpu.VMEM((tm, M_V), BF16)],
        compiler_params=_params(("parallel", "arbitrary")),
        name="merge",
    )(hf, hb, proj, g_mhead, oa, proj, proj, b_merge, b_merge, wm, wa)


def _out_proj_kernel(z_ref, w_ref, h_ref, o_ref):
    o_ref[...] = h_ref[...] + jnp.dot(z_ref[...], w_ref[...], preferred_element_type=F32)


def _out_proj(z, w, h):
    T = z.shape[0]
    tm = _pick_tile(T, 1024, 256)
    tn = 512
    return pl.pallas_call(
        _out_proj_kernel,
        grid=(T // tm, D_MODEL // tn),
        in_specs=[pl.BlockSpec((tm, D_MODEL), lambda i, j: (i, 0)),
                  pl.BlockSpec((D_MODEL, tn), lambda i, j: (0, j)),
                  pl.BlockSpec((tm, tn), lambda i, j: (i, j))],
        out_specs=pl.BlockSpec((tm, tn), lambda i, j: (i, j)),
        out_shape=jax.ShapeDtypeStruct((T, D_MODEL), F32),
        compiler_params=_params(("parallel", "arbitrary")),
        name="out_proj",
    )(z, w, h)


N_RANK = P_TOPK + 1


def _top_values(s, count):
    vals = []
    for _ in range(count):
        m = jnp.max(s, axis=0, keepdims=True)
        vals.append(m)
        s = jnp.where(s >= m, -jnp.inf, s)
    return vals


def _peer_score_kernel(h_ref, g_ref, wq_ref, k1_ref, k2_ref, xt_ref, thr_ref, c_ref, s2_ref, e2_ref):
    x = h_ref[...]
    x = x * lax.rsqrt(jnp.mean(x * x, axis=-1, keepdims=True) + EPS) * g_ref[...]
    xt = x.T.astype(BF16)
    xt_ref[...] = xt
    qt = jnp.dot(wq_ref[...], xt, preferred_element_type=F32)
    half = P_DQ // 2
    for h in range(P_HEADS):
        q1 = qt[h * P_DQ:h * P_DQ + half].astype(BF16)
        q2 = qt[h * P_DQ + half:(h + 1) * P_DQ].astype(BF16)
        s1 = jnp.dot(k1_ref[h], q1, preferred_element_type=F32)
        s2 = jnp.dot(k2_ref[h], q2, preferred_element_type=F32)
        v1 = _top_values(s1, N_RANK)
        v2 = _top_values(s2, N_RANK)
        cand = jnp.concatenate([v1[i] + v2[j] for i in range(N_RANK)
                                for j in range(N_RANK // (i + 1))], axis=0)
        top = _top_values(cand, N_RANK)
        c16, c17 = top[P_TOPK - 1], top[P_TOPK]
        tau = 0.5 * (c16 + c17)
        cmax = v1[0] + v2[0]
        z = jnp.sum(jnp.where(cand >= c16, jnp.exp(cand - cmax), 0.0), axis=0, keepdims=True)
        sl = slice(h * P_NKEYS, (h + 1) * P_NKEYS)
        thr_ref[sl, :] = tau - s1
        c_ref[sl, :] = jnp.exp(s1 - v1[0]) / z
        s2_ref[sl, :] = s2
        e2_ref[sl, :] = jnp.exp(s2 - v2[0])


def _peer_scores(h, g, wq_t, k1, k2):
    T = h.shape[0]
    tm = _pick_tile(T, 256, 128)
    HK = P_HEADS * P_NKEYS
    tok = lambda i: (0, i)
    side = jax.ShapeDtypeStruct((HK, T), F32)
    return pl.pallas_call(
        _peer_score_kernel,
        grid=(T // tm,),
        in_specs=[pl.BlockSpec((tm, D_MODEL), lambda i: (i, 0)),
                  pl.BlockSpec((1, D_MODEL), lambda i: (0, 0)),
                  pl.BlockSpec((P_HEADS * P_DQ, D_MODEL), lambda i: (0, 0)),
                  pl.BlockSpec((P_HEADS, P_NKEYS, P_DQ // 2), lambda i: (0, 0, 0)),
                  pl.BlockSpec((P_HEADS, P_NKEYS, P_DQ // 2), lambda i: (0, 0, 0))],
        out_specs=[pl.BlockSpec((D_MODEL, tm), tok)] + [pl.BlockSpec((HK, tm), tok)] * 4,
        out_shape=[jax.ShapeDtypeStruct((D_MODEL, T), BF16), side, side, side, side],
        compiler_params=_params(("parallel",)),
        name="peer_scores",
    )(h, g, wq_t, k1, k2)


def _peer_expert_kernel(xt_ref, thr_ref, c_ref, s2_ref, e2_ref, u_ref, vt_ref, h_ref, o_ref, acc_ref):
    j = pl.program_id(1)
    te = u_ref.shape[0]
    na = te // P_NKEYS

    @pl.when(j == 0)
    def _():
        acc_ref[...] = jnp.zeros_like(acc_ref)

    a_t = jnp.dot(u_ref[...], xt_ref[...], preferred_element_type=F32)
    act = 0.5 * a_t * (1.0 + lax.erf(a_t * (2.0 ** -0.5)))
    blocks = []
    for al in range(na):
        a = j * na + al
        gate = None
        for h in range(P_HEADS):
            thr = thr_ref[pl.ds(h * P_NKEYS + a, 1), :]
            coef = c_ref[pl.ds(h * P_NKEYS + a, 1), :]
            sl = slice(h * P_NKEYS, (h + 1) * P_NKEYS)
            term = jnp.where(s2_ref[sl, :] >= thr, e2_ref[sl, :] * coef, 0.0)
            gate = term if gate is None else gate + term
        blocks.append(gate * act[al * P_NKEYS:(al + 1) * P_NKEYS])
    w = jnp.concatenate(blocks, axis=0).astype(BF16)
    acc_ref[...] += jnp.dot(vt_ref[...], w, preferred_element_type=F32)

    @pl.when(j == pl.num_programs(1) - 1)
    def _():
        o_ref[...] = h_ref[...] + acc_ref[...].T


def _peer_experts(xt, thr, coef, s2, e2, u, vt, h):
    T = h.shape[0]
    tm = _pick_tile(T, 256, 128)
    te = 512
    HK = P_HEADS * P_NKEYS
    tok = lambda i, j: (0, i)
    return pl.pallas_call(
        _peer_expert_kernel,
        grid=(T // tm, P_NEXP // te),
        in_specs=[pl.BlockSpec((D_MODEL, tm), tok)] + [pl.BlockSpec((HK, tm), tok)] * 4
                 + [pl.BlockSpec((te, D_MODEL), lambda i, j: (j, 0)),
                    pl.BlockSpec((D_MODEL, te), lambda i, j: (0, j)),
                    pl.BlockSpec((tm, D_MODEL), lambda i, j: (i, 0))],
        out_specs=pl.BlockSpec((tm, D_MODEL), lambda i, j: (i, 0)),
        out_shape=jax.ShapeDtypeStruct((T, D_MODEL), F32),
        scratch_shapes=[pltpu.VMEM((D_MODEL, tm), F32)],
        compiler_params=_params(("parallel", "arbitrary")),
        name="peer_experts",
    )(xt, thr, coef, s2, e2, u, vt, h)


def _prep_layer(l, g_mix, w_in, b_mgate, g_mhead, g_q, g_k, attn_sink, w_branch_m, w_branch_a,
                b_merge, w_out, g_ffn, w_pq, sub_k1, sub_k2, u_tab, v_tab):
    n_gate0 = OFF_MO + M_V
    w = w_in[l]
    w_main = jnp.concatenate([w[:, :n_gate0], w[:, n_gate0 + N_GATE:]], axis=1).astype(BF16)
    w_gate = jnp.pad(w[:, n_gate0:n_gate0 + N_GATE], ((0, 0), (0, LANES - N_GATE))).astype(BF16)
    bias = jnp.pad(b_mgate[l], (0, LANES - N_GATE)).reshape(1, LANES)
    return dict(
        g_mix=g_mix[l].reshape(1, D_MODEL), w_main=w_main, w_gate=w_gate, bias=bias,
        g_mhead=g_mhead[l].reshape(1, M_V), g_q=g_q[l].reshape(1, A_DH), g_k=g_k[l].reshape(1, A_DH),
        sink=attn_sink[l], wm=w_branch_m[l].astype(BF16), wa=w_branch_a[l].astype(BF16),
        b_merge=b_merge[l].reshape(1, 2 * D_MODEL), wo=w_out[l].astype(BF16),
        g_ffn=g_ffn[l].reshape(1, D_MODEL), wq_t=w_pq[l].T.astype(BF16),
        k1=sub_k1[l].astype(BF16), k2=sub_k2[l].astype(BF16),
        u=u_tab[l].astype(BF16), vt=v_tab[l].T.astype(BF16))


def _layer(h, p, cos, sin, B, nc):
    proj = _norm_proj(h, p["g_mix"], p["w_main"], 1024)
    gates = _norm_proj(h, p["g_mix"], p["w_gate"], LANES)
    hf, hb = _mlstm(proj, gates, p["bias"], B, nc)
    q, k, v = _qk_prep(proj, cos, sin, p["g_q"], p["g_k"])
    oa = _attention(q, k, v, p["sink"], B, nc)
    z = _merge(hf, hb, proj, p["g_mhead"], oa, p["b_merge"], p["wm"], p["wa"])
    h = _out_proj(z, p["wo"], h)
    xt, thr, coef, s2, e2 = _peer_scores(h, p["g_ffn"], p["wq_t"], p["k1"], p["k2"])
    return _peer_experts(xt, thr, coef, s2, e2, p["u"], p["vt"], h)


def _trunk(x, meta_tokens, layers):
    B, L, D = x.shape
    Lp = BLOCK + L
    nc = Lp // BLOCK
    meta = jnp.broadcast_to(meta_tokens.astype(x.dtype)[None], (B, N_META, D))
    h = jnp.concatenate([jnp.zeros((B, META_START, D), x.dtype), meta, x], axis=1).reshape(B * Lp, D)
    pos = (jnp.arange(Lp) - META_START).astype(F32)
    half = A_DH // 2
    inv = ROPE_THETA ** (-jnp.arange(half, dtype=F32) / half)
    ang = pos[:, None] * inv[None, :]
    cos = jnp.tile(jnp.concatenate([jnp.cos(ang), jnp.cos(ang)], axis=1), (B, 1))
    sin = jnp.tile(jnp.concatenate([-jnp.sin(ang), jnp.sin(ang)], axis=1), (B, 1))
    for p in layers:
        h = _layer(h, p, cos, sin, B, nc)
    return h.reshape(B, Lp, D)[:, BLOCK:]


def kernel(x_prompt, x_sample, meta_tokens, g_mix, w_in, b_mgate, g_mhead, g_q, g_k, attn_sink,
           w_branch_m, w_branch_a, b_merge, w_out, g_ffn, w_pq, sub_k1, sub_k2, u_tab, v_tab):
    weights = (g_mix, w_in, b_mgate, g_mhead, g_q, g_k, attn_sink, w_branch_m, w_branch_a,
               b_merge, w_out, g_ffn, w_pq, sub_k1, sub_k2, u_tab, v_tab)
    layers = [_prep_layer(l, *weights) for l in range(w_in.shape[0])]
    return (_trunk(x_prompt, meta_tokens, layers), _trunk(x_sample, meta_tokens, layers))
```

```python
import functools

import jax
import jax.numpy as jnp
from jax import lax
from jax.experimental import pallas as pl
from jax.experimental.pallas import tpu as pltpu

F32 = jnp.float32
BF16 = jnp.bfloat16

D_MODEL = 2048
DEPTH = 4
N_META = 16
BLOCK = 128
META_START = BLOCK - N_META
EPS = 1e-6
M_HEADS = 8
M_DQK = 128
M_DV = 256
A_HEADS = 16
A_KV_HEADS = 4
A_GROUP = A_HEADS // A_KV_HEADS
A_DH = 128
WINDOW = 128
ROPE_THETA = 10000.0
P_HEADS = 8
P_NKEYS = 128
P_NEXP = P_NKEYS * P_NKEYS
P_DQ = 256
P_TOPK = 16

M_QK = M_HEADS * M_DQK
M_V = M_HEADS * M_DV
A_Q = A_HEADS * A_DH
A_KV = A_KV_HEADS * A_DH
OFF_MQ = 0
OFF_MK = OFF_MQ + M_QK
OFF_MV = OFF_MK + M_QK
OFF_MO = OFF_MV + M_V
OFF_AQ = OFF_MO + M_V
OFF_AK = OFF_AQ + A_Q
OFF_AV = OFF_AK + A_KV
OFF_GM = OFF_AV + A_KV
N_MAIN = OFF_GM + 2 * D_MODEL
N_GATE = 4 * M_HEADS
LANES = 128
STATE_W = M_DV + LANES
VMEM_LIMIT = 56 * 1024 * 1024


def _pick_tile(n, cap, mult):
    best = None
    for t in range(mult, min(n, cap) + 1, mult):
        if n % t == 0:
            best = t
    assert best is not None, (n, cap, mult)
    return best


def _params(sem):
    return pltpu.CompilerParams(dimension_semantics=sem, vmem_limit_bytes=VMEM_LIMIT)


def _norm_proj_kernel(h_ref, g_ref, w_ref, o_ref, xn_ref):
    @pl.when(pl.program_id(1) == 0)
    def _():
        x = h_ref[...]
        ms = jnp.mean(x * x, axis=-1, keepdims=True)
        xn_ref[...] = (x * lax.rsqrt(ms + EPS) * g_ref[...]).astype(BF16)

    o_ref[...] = jnp.dot(xn_ref[...], w_ref[...], preferred_element_type=F32)


def _norm_proj(h, g, w, tn):
    T, D = h.shape
    N = w.shape[1]
    tm = _pick_tile(T, 1024, 256)
    return pl.pallas_call(
        _norm_proj_kernel,
        grid=(T // tm, N // tn),
        in_specs=[pl.BlockSpec((tm, D), lambda i, j: (i, 0)),
                  pl.BlockSpec((1, D), lambda i, j: (0, 0)),
                  pl.BlockSpec((D, tn), lambda i, j: (0, j))],
        out_specs=pl.BlockSpec((tm, tn), lambda i, j: (i, j)),
        out_shape=jax.ShapeDtypeStruct((T, N), F32),
        scratch_shapes=[pltpu.VMEM((tm, D), BF16)],
        compiler_params=_params(("parallel", "arbitrary")),
        name="norm_proj",
    )(h, g, w)


def _mlstm_direction(q_ref, k_ref, v_ref, g_ref, bias_ref, o_ref, s_ref, m_ref, tok0, fwd):
    T = BLOCK
    g = g_ref[...] + bias_ref[...]
    rows = lax.broadcasted_iota(jnp.int32, (T, LANES), 0)
    valid = (tok0 + rows) >= META_START
    logsig = jnp.minimum(g, 0.0) - jnp.log1p(jnp.exp(-jnp.abs(g)))
    lf = jnp.where(valid, logsig, 0.0)
    li = jnp.where(valid, g, -jnp.inf)
    r = lax.broadcasted_iota(jnp.int32, (T, T), 0)
    c = lax.broadcasted_iota(jnp.int32, (T, T), 1)
    tri = (c <= r) if fwd else (c >= r)
    bcum = jnp.dot(tri.astype(F32), lf, preferred_element_type=F32,
                   precision=lax.Precision.HIGHEST)
    bcum_t = bcum.T
    li_t = li.T
    i_off = 0 if fwd else 2 * M_HEADS
    f_off = i_off + M_HEADS
    last = T - 1 if fwd else 0
    ones = jnp.ones((T, LANES), F32)
    for h in range(M_HEADS):
        b_col = bcum[:, f_off + h:f_off + h + 1]
        b_row = bcum_t[f_off + h:f_off + h + 1, :]
        li_col = li[:, i_off + h:i_off + h + 1]
        li_row = li_t[i_off + h:i_off + h + 1, :]
        b_last = bcum[last:last + 1, f_off + h:f_off + h + 1]
        m_old = m_ref[h:h + 1, 0:1]
        qb = q_ref[:, h * M_DQK:(h + 1) * M_DQK].astype(BF16)
        kb = (k_ref[:, h * M_DQK:(h + 1) * M_DQK] * (M_DQK ** -0.5)).astype(BF16)
        v_ext = jnp.concatenate([v_ref[:, h * M_DV:(h + 1) * M_DV], ones], axis=1)
        state = s_ref[h]

        dm = jnp.where(tri, b_col - b_row + li_row, -jnp.inf)
        inter = b_col + m_old
        m_t = jnp.maximum(inter, jnp.max(dm, axis=1, keepdims=True))
        s_qk = lax.dot_general(qb, kb, (((1,), (1,)), ((), ())), preferred_element_type=F32)
        p = (jnp.exp(dm - m_t) * s_qk).astype(BF16)
        w_inter = jnp.exp(inter - m_t)
        nd = (jnp.dot(p, v_ext.astype(BF16), preferred_element_type=F32)
              + w_inter * jnp.dot(qb, state.astype(BF16), preferred_element_type=F32))
        den = jnp.maximum(jnp.abs(nd[:, M_DV:]), jnp.exp(-m_t))
        inv = 1.0 / den
        o_ref[:, h * M_DV:(h + 1) * M_DV] = nd[:, :M_DV] * jnp.concatenate([inv, inv], axis=1)

        gk = b_last - b_col + li_col
        m_new = jnp.maximum(b_last + m_old, jnp.max(gk, axis=0, keepdims=True))
        wk = jnp.exp(gk - m_new)
        decay = jnp.exp(b_last + m_old - m_new)
        upd = lax.dot_general(kb, (wk * v_ext).astype(BF16), (((0,), (0,)), ((), ())),
                              preferred_element_type=F32)
        s_ref[h] = decay * state + upd
        m_ref[h:h + 1, :] = jnp.broadcast_to(m_new, (1, LANES))


def _mlstm_kernel(qf, kf, vf, gf, qb, kb, vb, gb, bias, of, ob, sf, sb, mf, mb):
    c = pl.program_id(1)
    nc = pl.num_programs(1)

    @pl.when(c == 0)
    def _():
        sf[...] = jnp.zeros_like(sf)
        sb[...] = jnp.zeros_like(sb)
        mf[...] = jnp.zeros_like(mf)
        mb[...] = jnp.zeros_like(mb)

    _mlstm_direction(qf, kf, vf, gf, bias, of, sf, mf, c * BLOCK, True)
    _mlstm_direction(qb, kb, vb, gb, bias, ob, sb, mb, (nc - 1 - c) * BLOCK, False)


def _mlstm(proj, gates, bias, B, nc):
    T = proj.shape[0]

    def fwd_map(col):
        return lambda b, c: (b * nc + c, col)

    def bwd_map(col):
        return lambda b, c: (b * nc + nc - 1 - c, col)

    def specs(mk):
        return [pl.BlockSpec((BLOCK, M_QK), mk(OFF_MQ // M_QK)),
                pl.BlockSpec((BLOCK, M_QK), mk(OFF_MK // M_QK)),
                pl.BlockSpec((BLOCK, M_V), mk(OFF_MV // M_V))]

    gate_f = pl.BlockSpec((BLOCK, LANES), fwd_map(0))
    gate_b = pl.BlockSpec((BLOCK, LANES), bwd_map(0))
    out = jax.ShapeDtypeStruct((T, M_V), F32)
    return pl.pallas_call(
        _mlstm_kernel,
        grid=(B, nc),
        in_specs=(specs(fwd_map) + [gate_f] + specs(bwd_map) + [gate_b]
                  + [pl.BlockSpec((1, LANES), lambda b, c: (0, 0))]),
        out_specs=[pl.BlockSpec((BLOCK, M_V), fwd_map(0)), pl.BlockSpec((BLOCK, M_V), bwd_map(0))],
        out_shape=[out, out],
        scratch_shapes=[pltpu.VMEM((M_HEADS, M_DQK, STATE_W), F32),
                        pltpu.VMEM((M_HEADS, M_DQK, STATE_W), F32),
                        pltpu.VMEM((M_HEADS, LANES), F32),
                        pltpu.VMEM((M_HEADS, LANES), F32)],
        compiler_params=_params(("parallel", "arbitrary")),
        name="mlstm",
    )(proj, proj, proj, gates, proj, proj, proj, gates, bias)


def _qk_prep_kernel(q_ref, k_ref, v_ref, cos_ref, sin_ref, gq_ref, gk_ref, qo_ref, ko_ref, vo_ref):
    cos = cos_ref[...]
    sin = sin_ref[...]

    def norm_rope(x, g):
        xn = x * lax.rsqrt(jnp.mean(x * x, axis=-1, keepdims=True) + EPS) * g
        return xn * cos + pltpu.roll(xn, A_DH // 2, axis=1) * sin

    for h in range(A_HEADS):
        sl = slice(h * A_DH, (h + 1) * A_DH)
        qo_ref[:, sl] = (norm_rope(q_ref[:, sl], gq_ref[...]) * (A_DH ** -0.5)).astype(BF16)
    for h in range(A_KV_HEADS):
        sl = slice(h * A_DH, (h + 1) * A_DH)
        ko_ref[:, sl] = norm_rope(k_ref[:, sl], gk_ref[...]).astype(BF16)
    vo_ref[...] = v_ref[...].astype(BF16)


def _qk_prep(proj, cos, sin, g_q, g_k):
    T = proj.shape[0]
    tm = _pick_tile(T, 512, 128)
    row = lambda col: (lambda i: (i, col))
    const = lambda i: (0, 0)
    return pl.pallas_call(
        _qk_prep_kernel,
        grid=(T // tm,),
        in_specs=[pl.BlockSpec((tm, A_Q), row(OFF_AQ // A_Q)),
                  pl.BlockSpec((tm, A_KV), row(OFF_AK // A_KV)),
                  pl.BlockSpec((tm, A_KV), row(OFF_AV // A_KV)),
                  pl.BlockSpec((tm, A_DH), row(0)),
                  pl.BlockSpec((tm, A_DH), row(0)),
                  pl.BlockSpec((1, A_DH), const),
                  pl.BlockSpec((1, A_DH), const)],
        out_specs=[pl.BlockSpec((tm, A_Q), row(0)),
                   pl.BlockSpec((tm, A_KV), row(0)),
                   pl.BlockSpec((tm, A_KV), row(0))],
        out_shape=[jax.ShapeDtypeStruct((T, A_Q), BF16),
                   jax.ShapeDtypeStruct((T, A_KV), BF16),
                   jax.ShapeDtypeStruct((T, A_KV), BF16)],
        compiler_params=_params(("parallel",)),
        name="qk_prep",
    )(proj, proj, proj, cos, sin, g_q, g_k)


def _attn_kernel(sink_ref, q_ref, kp, kc, kn, km, vp, vc, vn, vm, o_ref):
    n = pl.program_id(1)
    nb = pl.num_programs(1)
    R = A_GROUP * BLOCK
    W = 4 * BLOCK
    row = lax.broadcasted_iota(jnp.int32, (R, W), 0)
    col = lax.broadcasted_iota(jnp.int32, (R, W), 1)
    t = row % BLOCK
    seg = col // BLOCK
    off = col % BLOCK
    dist = (1 - seg) * BLOCK + (t - off)
    blk = n - 1 + seg
    win_ok = (seg < 3) & (jnp.abs(dist) <= WINDOW) & (blk >= 1) & (blk <= nb - 1)
    meta_ok = (seg == 3) & (off >= META_START)
    mask = win_ok | meta_ok
    head_row = lax.broadcasted_iota(jnp.int32, (R, 1), 0) // BLOCK
    for j in range(A_KV_HEADS):
        sl = slice(j * A_DH, (j + 1) * A_DH)
        q4 = jnp.concatenate([q_ref[:, (j * A_GROUP + g) * A_DH:(j * A_GROUP + g + 1) * A_DH]
                              for g in range(A_GROUP)], axis=0)
        kcat = jnp.concatenate([kp[:, sl], kc[:, sl], kn[:, sl], km[:, sl]], axis=0)
        vcat = jnp.concatenate([vp[:, sl], vc[:, sl], vn[:, sl], vm[:, sl]], axis=0)
        s = lax.dot_general(q4, kcat, (((1,), (1,)), ((), ())), preferred_element_type=F32)
        s = jnp.where(mask, s, -jnp.inf)
        sink = jnp.zeros((R, 1), F32)
        for g in range(A_GROUP):
            sink = jnp.where(head_row == g, sink_ref[j * A_GROUP + g], sink)
        m = jnp.maximum(jnp.max(s, axis=1, keepdims=True), sink)
        p = jnp.exp(s - m)
        denom = jnp.sum(p, axis=1, keepdims=True) + jnp.exp(sink - m)
        o = jnp.dot(p.astype(BF16), vcat, preferred_element_type=F32) / denom
        for g in range(A_GROUP):
            hh = j * A_GROUP + g
            o_ref[:, hh * A_DH:(hh + 1) * A_DH] = o[g * BLOCK:(g + 1) * BLOCK].astype(BF16)


def _attention(q, k, v, sink, B, nb):
    T = q.shape[0]
    cur = lambda b, n, s: (b * nb + n, 0)
    prev = lambda b, n, s: (b * nb + jnp.maximum(n - 1, 0), 0)
    nxt = lambda b, n, s: (b * nb + jnp.minimum(n + 1, nb - 1), 0)
    meta = lambda b, n, s: (b * nb, 0)
    kv = lambda m: pl.BlockSpec((BLOCK, A_KV), m)
    grid_spec = pltpu.PrefetchScalarGridSpec(
        num_scalar_prefetch=1,
        grid=(B, nb),
        in_specs=[pl.BlockSpec((BLOCK, A_Q), cur),
                  kv(prev), kv(cur), kv(nxt), kv(meta), kv(prev), kv(cur), kv(nxt), kv(meta)],
        out_specs=pl.BlockSpec((BLOCK, A_Q), cur))
    return pl.pallas_call(
        _attn_kernel,
        grid_spec=grid_spec,
        out_shape=jax.ShapeDtypeStruct((T, A_Q), BF16),
        compiler_params=_params(("parallel", "arbitrary")),
        name="band_attn",
    )(sink, q, k, k, k, k, v, v, v, v)


def _merge_kernel(hf_ref, hb_ref, mo_ref, gh_ref, oa_ref, gmm_ref, gma_ref, bm_ref, ba_ref,
                  wm_ref, wa_ref, z_ref, hm_ref):
    @pl.when(pl.program_id(1) == 0)
    def _():
        for h in range(M_HEADS):
            sl = slice(h * M_DV, (h + 1) * M_DV)
            x = hf_ref[:, sl] + hb_ref[:, sl]
            xn = x * lax.rsqrt(jnp.mean(x * x, axis=-1, keepdims=True) + EPS) * gh_ref[:, sl]
            hm_ref[:, sl] = (xn * jax.nn.sigmoid(mo_ref[:, sl])).astype(BF16)

    y_m = jnp.dot(hm_ref[...], wm_ref[...], preferred_element_type=F32)
    y_a = jnp.dot(oa_ref[...], wa_ref[...], preferred_element_type=F32)
    gate_m = jax.nn.sigmoid(gmm_ref[...] + bm_ref[...])
    gate_a = jax.nn.sigmoid(gma_ref[...] + ba_ref[...])
    z_ref[...] = (gate_m * y_m + gate_a * y_a).astype(BF16)


def _merge(hf, hb, proj, g_mhead, oa, b_merge, wm, wa):
    T = hf.shape[0]
    tm = _pick_tile(T, 512, 128)
    tn = 512
    nj = D_MODEL // tn
    full = lambda col: (lambda i, j: (i, col))
    return pl.pallas_call(
        _merge_kernel,
        grid=(T // tm, nj),
        in_specs=[pl.BlockSpec((tm, M_V), full(0)),
                  pl.BlockSpec((tm, M_V), full(0)),
                  pl.BlockSpec((tm, M_V), full(OFF_MO // M_V)),
                  pl.BlockSpec((1, M_V), lambda i, j: (0, 0)),
                  pl.BlockSpec((tm, A_Q), full(0)),
                  pl.BlockSpec((tm, tn), lambda i, j: (i, OFF_GM // tn + j)),
                  pl.BlockSpec((tm, tn), lambda i, j: (i, OFF_GM // tn + nj + j)),
                  pl.BlockSpec((1, tn), lambda i, j: (0, j)),
                  pl.BlockSpec((1, tn), lambda i, j: (0, nj + j)),
                  pl.BlockSpec((M_V, tn), lambda i, j: (0, j)),
                  pl.BlockSpec((A_Q, tn), lambda i, j: (0, j))],
        out_specs=pl.BlockSpec((tm, tn), lambda i, j: (i, j)),
        out_shape=jax.ShapeDtypeStruct((T, D_MODEL), BF16),
        scratch_shapes=[pltpu.VMEM((tm, M_V), BF16)],
        compiler_params=_params(("parallel", "arbitrary")),
        name="merge",
    )(hf, hb, proj, g_mhead, oa, proj, proj, b_merge, b_merge, wm, wa)


def _out_proj_kernel(z_ref, w_ref, h_ref, o_ref):
    o_ref[...] = h_ref[...] + jnp.dot(z_ref[...], w_ref[...], preferred_element_type=F32)


def _out_proj(z, w, h):
    T = z.shape[0]
    tm = _pick_tile(T, 1024, 256)
    tn = 512
    return pl.pallas_call(
        _out_proj_kernel,
        grid=(T // tm, D_MODEL // tn),
        in_specs=[pl.BlockSpec((tm, D_MODEL), lambda i, j: (i, 0)),
                  pl.BlockSpec((D_MODEL, tn), lambda i, j: (0, j)),
                  pl.BlockSpec((tm, tn), lambda i, j: (i, j))],
        out_specs=pl.BlockSpec((tm, tn), lambda i, j: (i, j)),
        out_shape=jax.ShapeDtypeStruct((T, D_MODEL), F32),
        compiler_params=_params(("parallel", "arbitrary")),
        name="out_proj",
    )(z, w, h)


def _top_ranked(s, count):
    vals = []
    rank = jnp.full(s.shape, float(count), F32)
    for i in range(count):
        m = jnp.max(s, axis=0, keepdims=True)
        vals.append(m)
        hit = s >= m
        rank = jnp.where(hit, float(i), rank)
        s = jnp.where(hit, -jnp.inf, s)
    return vals, rank


def _peer_score_kernel(h_ref, g_ref, wq_ref, k1_ref, k2_ref, xt_ref, n_ref, c_ref, r2_ref, e2_ref):
    x = h_ref[...]
    x = x * lax.rsqrt(jnp.mean(x * x, axis=-1, keepdims=True) + EPS) * g_ref[...]
    xt = x.T.astype(BF16)
    xt_ref[...] = xt
    qt = jnp.dot(wq_ref[...], xt, preferred_element_type=F32)
    half = P_DQ // 2
    for h in range(P_HEADS):
        q1 = qt[h * P_DQ:h * P_DQ + half].astype(BF16)
        q2 = qt[h * P_DQ + half:(h + 1) * P_DQ].astype(BF16)
        s1 = jnp.dot(k1_ref[h], q1, preferred_element_type=F32)
        s2 = jnp.dot(k2_ref[h], q2, preferred_element_type=F32)
        v1, r1 = _top_ranked(s1, P_TOPK)
        v2, r2 = _top_ranked(s2, P_TOPK)
        rows = [[v1[i] + v2[j] for j in range(P_TOPK // (i + 1))] for i in range(P_TOPK)]
        cand = jnp.concatenate([c for row in rows for c in row], axis=0)
        s = cand
        for _ in range(P_TOPK):
            c16 = jnp.max(s, axis=0, keepdims=True)
            s = jnp.where(s >= c16, -jnp.inf, s)
        cmax = v1[0] + v2[0]
        z = jnp.sum(jnp.where(cand >= c16, jnp.exp(cand - cmax), 0.0), axis=0, keepdims=True)
        n_sel = jnp.zeros_like(s1)
        for i in range(P_TOPK):
            cnt = sum((c >= c16).astype(F32) for c in rows[i])
            n_sel = jnp.where(r1 == float(i), cnt, n_sel)
        sl = slice(h * P_NKEYS, (h + 1) * P_NKEYS)
        n_ref[sl, :] = n_sel
        c_ref[sl, :] = jnp.exp(s1 - v1[0]) / z
        r2_ref[sl, :] = r2.astype(BF16)
        e2_ref[sl, :] = jnp.exp(s2 - v2[0]).astype(BF16)


def _peer_scores(h, g, wq_t, k1, k2):
    T = h.shape[0]
    tm = _pick_tile(T, 256, 128)
    HK = P_HEADS * P_NKEYS
    tok = lambda i: (0, i)
    f32_side = jax.ShapeDtypeStruct((HK, T), F32)
    bf16_side = jax.ShapeDtypeStruct((HK, T), BF16)
    return pl.pallas_call(
        _peer_score_kernel,
        grid=(T // tm,),
        in_specs=[pl.BlockSpec((tm, D_MODEL), lambda i: (i, 0)),
                  pl.BlockSpec((1, D_MODEL), lambda i: (0, 0)),
                  pl.BlockSpec((P_HEADS * P_DQ, D_MODEL), lambda i: (0, 0)),
                  pl.BlockSpec((P_HEADS, P_NKEYS, P_DQ // 2), lambda i: (0, 0, 0)),
                  pl.BlockSpec((P_HEADS, P_NKEYS, P_DQ // 2), lambda i: (0, 0, 0))],
        out_specs=[pl.BlockSpec((D_MODEL, tm), tok)] + [pl.BlockSpec((HK, tm), tok)] * 4,
        out_shape=[jax.ShapeDtypeStruct((D_MODEL, T), BF16), f32_side, f32_side, bf16_side, bf16_side],
        compiler_params=_params(("parallel",)),
        name="peer_scores",
    )(h, g, wq_t, k1, k2)


PEER_TE = 1024
PEER_NJ = P_NEXP // PEER_TE
PEER_DROWS = 512


def _peer_expert_kernel(xt_ref, n_ref, c_ref, r2_ref, e2_ref, u_ref, vt_ref, o_ref, w0_ref, w1_ref, a_ref):
    j = pl.program_id(1)
    nj = pl.num_programs(1) - 1
    na = PEER_TE // P_NKEYS
    w_ref = (w0_ref, w1_ref)

    def gate_block(a_t, slot, al):
        a = j * na + al
        gate = None
        for h in range(P_HEADS):
            n_sel = n_ref[pl.ds(h * P_NKEYS + a, 1), :].astype(BF16)
            coef = c_ref[pl.ds(h * P_NKEYS + a, 1), :].astype(BF16)
            sl = slice(h * P_NKEYS, (h + 1) * P_NKEYS)
            term = jnp.where(r2_ref[sl, :] < n_sel, e2_ref[sl, :], 0.0) * coef
            gate = term if gate is None else gate + term
        x = a_t[al * P_NKEYS:(al + 1) * P_NKEYS]
        act = 0.5 * x * (1.0 + lax.erf(x * (2.0 ** -0.5)))
        w_ref[slot][al * P_NKEYS:(al + 1) * P_NKEYS, :] = gate * act.astype(BF16)

    def accumulate(slot, k):
        rows = slice(k * PEER_DROWS, (k + 1) * PEER_DROWS)
        o_ref[rows, :] += jnp.dot(vt_ref[rows, :], w_ref[slot][...], preferred_element_type=F32)

    n_acc = D_MODEL // PEER_DROWS

    @pl.when(j == 0)
    def _():
        o_ref[...] = jnp.zeros_like(o_ref)
        w1_ref[...] = jnp.zeros_like(w1_ref)

    for parity in range(2):
        @pl.when((j < nj) & (j % 2 == parity))
        def _():
            a_ref[...] = jnp.dot(u_ref[...], xt_ref[...], preferred_element_type=F32)
            for k in range(n_acc):
                for al in range(k * na // n_acc, (k + 1) * na // n_acc):
                    gate_block(a_ref, parity, al)
                accumulate(1 - parity, k)

    @pl.when(j == nj)
    def _():
        for k in range(n_acc):
            accumulate((PEER_NJ - 1) % 2, k)


def _peer_experts(xt, n_sel, coef, r2, e2, u, vt):
    T = xt.shape[1]
    tm = _pick_tile(T, 768, 256)
    te = PEER_TE
    nj = PEER_NJ
    HK = P_HEADS * P_NKEYS
    once = pl.Buffered(1)
    tok = lambda i, j: (0, i)
    return pl.pallas_call(
        _peer_expert_kernel,
        grid=(T // tm, nj + 1),
        in_specs=[pl.BlockSpec((D_MODEL, tm), tok, pipeline_mode=once)]
                 + [pl.BlockSpec((HK, tm), tok, pipeline_mode=once)] * 4
                 + [pl.BlockSpec((te, D_MODEL), lambda i, j: (jnp.minimum(j, nj - 1), 0)),
                    pl.BlockSpec((D_MODEL, te), lambda i, j: (0, jnp.maximum(j - 1, 0)))],
        out_specs=pl.BlockSpec((D_MODEL, tm), tok),
        out_shape=jax.ShapeDtypeStruct((D_MODEL, T), F32),
        scratch_shapes=[pltpu.VMEM((te, tm), BF16), pltpu.VMEM((te, tm), BF16), pltpu.VMEM((te, tm), F32)],
        compiler_params=_params(("parallel", "arbitrary")),
        name="peer_experts",
    )(xt, n_sel, coef, r2, e2, u, vt)


def _add_transposed_kernel(h_ref, yt_ref, o_ref):
    o_ref[...] = h_ref[...] + yt_ref[...].T


def _add_transposed(h, yt):
    T, D = h.shape
    tm = _pick_tile(T, 512, 128)
    return pl.pallas_call(
        _add_transposed_kernel,
        grid=(T // tm,),
        in_specs=[pl.BlockSpec((tm, D), lambda i: (i, 0)), pl.BlockSpec((D, tm), lambda i: (0, i))],
        out_specs=pl.BlockSpec((tm, D), lambda i: (i, 0)),
        out_shape=jax.ShapeDtypeStruct((T, D), F32),
        compiler_params=_params(("parallel",)),
        name="add_transposed",
    )(h, yt)


def _prep_layer(l, g_mix, w_in, b_mgate, g_mhead, g_q, g_k, attn_sink, w_branch_m, w_branch_a,
                b_merge, w_out, g_ffn, w_pq, sub_k1, sub_k2, u_tab, v_tab):
    n_gate0 = OFF_MO + M_V
    w = w_in[l]
    w_main = jnp.concatenate([w[:, :n_gate0], w[:, n_gate0 + N_GATE:]], axis=1).astype(BF16)
    w_gate = jnp.pad(w[:, n_gate0:n_gate0 + N_GATE], ((0, 0), (0, LANES - N_GATE))).astype(BF16)
    bias = jnp.pad(b_mgate[l], (0, LANES - N_GATE)).reshape(1, LANES)
    return dict(
        g_mix=g_mix[l].reshape(1, D_MODEL), w_main=w_main, w_gate=w_gate, bias=bias,
        g_mhead=g_mhead[l].reshape(1, M_V), g_q=g_q[l].reshape(1, A_DH), g_k=g_k[l].reshape(1, A_DH),
        sink=attn_sink[l], wm=w_branch_m[l].astype(BF16), wa=w_branch_a[l].astype(BF16),
        b_merge=b_merge[l].reshape(1, 2 * D_MODEL), wo=w_out[l].astype(BF16),
        g_ffn=g_ffn[l].reshape(1, D_MODEL), wq_t=w_pq[l].T.astype(BF16),
        k1=sub_k1[l].astype(BF16), k2=sub_k2[l].astype(BF16),
        u=u_tab[l].astype(BF16), vt=v_tab[l].T.astype(BF16))


def _layer(h, p, cos, sin, B, nc):
    proj = _norm_proj(h, p["g_mix"], p["w_main"], 1024)
    gates = _norm_proj(h, p["g_mix"], p["w_gate"], LANES)
    hf, hb = _mlstm(proj, gates, p["bias"], B, nc)
    q, k, v = _qk_prep(proj, cos, sin, p["g_q"], p["g_k"])
    oa = _attention(q, k, v, p["sink"], B, nc)
    z = _merge(hf, hb, proj, p["g_mhead"], oa, p["b_merge"], p["wm"], p["wa"])
    h = _out_proj(z, p["wo"], h)
    xt, n_sel, coef, r2, e2 = _peer_scores(h, p["g_ffn"], p["wq_t"], p["k1"], p["k2"])
    return _add_transposed(h, _peer_experts(xt, n_sel, coef, r2, e2, p["u"], p["vt"]))


def _trunk(x, meta_tokens, layers):
    B, L, D = x.shape
    Lp = BLOCK + L
    nc = Lp // BLOCK
    meta = jnp.broadcast_to(meta_tokens.astype(x.dtype)[None], (B, N_META, D))
    h = jnp.concatenate([jnp.zeros((B, META_START, D), x.dtype), meta, x], axis=1).reshape(B * Lp, D)
    pos = (jnp.arange(Lp) - META_START).astype(F32)
    half = A_DH // 2
    inv = ROPE_THETA ** (-jnp.arange(half, dtype=F32) / half)
    ang = pos[:, None] * inv[None, :]
    cos = jnp.tile(jnp.concatenate([jnp.cos(ang), jnp.cos(ang)], axis=1), (B, 1))
    sin = jnp.tile(jnp.concatenate([-jnp.sin(ang), jnp.sin(ang)], axis=1), (B, 1))
    for p in layers:
        h = _layer(h, p, cos, sin, B, nc)
    return h.reshape(B, Lp, D)[:, BLOCK:]


def kernel(x_prompt, x_sample, meta_tokens, g_mix, w_in, b_mgate, g_mhead, g_q, g_k, attn_sink,
           w_branch_m, w_branch_a, b_merge, w_out, g_ffn, w_pq, sub_k1, sub_k2, u_tab, v_tab):
    weights = (g_mix, w_in, b_mgate, g_mhead, g_q, g_k, attn_sink, w_branch_m, w_branch_a,
               b_merge, w_out, g_ffn, w_pq, sub_k1, sub_k2, u_tab, v_tab)
    layers = [_prep_layer(l, *weights) for l in range(w_in.shape[0])]
    return (_trunk(x_prompt, meta_tokens, layers), _trunk(x_sample, meta_tokens, layers))
```

```python
import functools

import jax
import jax.numpy as jnp
from jax import lax
from jax.experimental import pallas as pl
from jax.experimental.pallas import tpu as pltpu

F32 = jnp.float32
BF16 = jnp.bfloat16

D_MODEL = 2048
DEPTH = 4
N_META = 16
BLOCK = 128
META_START = BLOCK - N_META
EPS = 1e-6
M_HEADS = 8
M_DQK = 128
M_DV = 256
A_HEADS = 16
A_KV_HEADS = 4
A_GROUP = A_HEADS // A_KV_HEADS
A_DH = 128
WINDOW = 128
ROPE_THETA = 10000.0
P_HEADS = 8
P_NKEYS = 128
P_NEXP = P_NKEYS * P_NKEYS
P_DQ = 256
P_TOPK = 16

M_QK = M_HEADS * M_DQK
M_V = M_HEADS * M_DV
A_Q = A_HEADS * A_DH
A_KV = A_KV_HEADS * A_DH
OFF_GM = 0
OFF_MQ = OFF_GM + 2 * D_MODEL
OFF_MK = OFF_MQ + M_QK
OFF_MV = OFF_MK + M_QK
OFF_MO = OFF_MV + M_V
OFF_AQ = OFF_MO + M_V
OFF_AK = OFF_AQ + A_Q
OFF_AV = OFF_AK + A_KV
N_MAIN = OFF_AV + A_KV
N_GATE = 4 * M_HEADS
LANES = 128
STATE_W = M_DV + LANES
VMEM_LIMIT = 56 * 1024 * 1024


def _pick_tile(n, cap, mult):
    best = None
    for t in range(mult, min(n, cap) + 1, mult):
        if n % t == 0:
            best = t
    assert best is not None, (n, cap, mult)
    return best


def _params(sem):
    return pltpu.CompilerParams(dimension_semantics=sem, vmem_limit_bytes=VMEM_LIMIT)


def _norm_proj_kernel(h_ref, g_ref, w_ref, o_ref, xn_ref):
    @pl.when(pl.program_id(1) == 0)
    def _():
        x = h_ref[...]
        ms = jnp.mean(x * x, axis=-1, keepdims=True)
        xn_ref[...] = (x * lax.rsqrt(ms + EPS) * g_ref[...]).astype(BF16)

    o_ref[...] = jnp.dot(xn_ref[...], w_ref[...], preferred_element_type=F32).astype(o_ref.dtype)


def _norm_proj(h, g, w, tn, out_dtype):
    T, D = h.shape
    N = w.shape[1]
    tm = _pick_tile(T, 1024, 256)
    return pl.pallas_call(
        _norm_proj_kernel,
        grid=(T // tm, N // tn),
        in_specs=[pl.BlockSpec((tm, D), lambda i, j: (i, 0)),
                  pl.BlockSpec((1, D), lambda i, j: (0, 0)),
                  pl.BlockSpec((D, tn), lambda i, j: (0, j))],
        out_specs=pl.BlockSpec((tm, tn), lambda i, j: (i, j)),
        out_shape=jax.ShapeDtypeStruct((T, N), out_dtype),
        scratch_shapes=[pltpu.VMEM((tm, D), BF16)],
        compiler_params=_params(("parallel", "arbitrary")),
        name="norm_proj",
    )(h, g, w)


def _mlstm_direction(q_ref, k_ref, v_ref, g_ref, bias_ref, o_ref, s_ref, m_ref, tok0, fwd):
    T = BLOCK
    g = g_ref[...] + bias_ref[...]
    rows = lax.broadcasted_iota(jnp.int32, (T, LANES), 0)
    valid = (tok0 + rows) >= META_START
    logsig = jnp.minimum(g, 0.0) - jnp.log1p(jnp.exp(-jnp.abs(g)))
    lf = jnp.where(valid, logsig, 0.0)
    li = jnp.where(valid, g, -jnp.inf)
    r = lax.broadcasted_iota(jnp.int32, (T, T), 0)
    c = lax.broadcasted_iota(jnp.int32, (T, T), 1)
    tri = (c <= r) if fwd else (c >= r)
    bcum = jnp.dot(tri.astype(F32), lf, preferred_element_type=F32,
                   precision=lax.Precision.HIGHEST)
    bcum_t = bcum.T
    li_t = li.T
    i_off = 0 if fwd else 2 * M_HEADS
    f_off = i_off + M_HEADS
    last = T - 1 if fwd else 0
    ones = jnp.ones((T, LANES), BF16)
    for h in range(M_HEADS):
        b_col = bcum[:, f_off + h:f_off + h + 1]
        b_row = bcum_t[f_off + h:f_off + h + 1, :]
        li_col = li[:, i_off + h:i_off + h + 1]
        li_row = li_t[i_off + h:i_off + h + 1, :]
        b_last = bcum[last:last + 1, f_off + h:f_off + h + 1]
        m_old = m_ref[h:h + 1, 0:1]
        qb = q_ref[:, h * M_DQK:(h + 1) * M_DQK].astype(BF16)
        kb = (k_ref[:, h * M_DQK:(h + 1) * M_DQK] * (M_DQK ** -0.5)).astype(BF16)
        v_ext = jnp.concatenate([v_ref[:, h * M_DV:(h + 1) * M_DV], ones], axis=1)
        state = s_ref[h]

        dm = jnp.where(tri, b_col - b_row + li_row, -jnp.inf)
        inter = b_col + m_old
        m_t = jnp.maximum(inter, jnp.max(dm, axis=1, keepdims=True))
        s_qk = lax.dot_general(qb, kb, (((1,), (1,)), ((), ())), preferred_element_type=F32)
        p = (jnp.exp(dm - m_t) * s_qk).astype(BF16)
        w_inter = jnp.exp(inter - m_t)
        nd = (jnp.dot(p, v_ext.astype(BF16), preferred_element_type=F32)
              + w_inter * jnp.dot(qb, state.astype(BF16), preferred_element_type=F32))
        den = jnp.maximum(jnp.abs(nd[:, M_DV:]), jnp.exp(-m_t))
        inv = 1.0 / den
        o_ref[:, h * M_DV:(h + 1) * M_DV] = (
            nd[:, :M_DV] * jnp.concatenate([inv, inv], axis=1)).astype(o_ref.dtype)

        gk = b_last - b_col + li_col
        m_new = jnp.maximum(b_last + m_old, jnp.max(gk, axis=0, keepdims=True))
        wk = jnp.exp(gk - m_new)
        decay = jnp.exp(b_last + m_old - m_new)
        upd = lax.dot_general(kb, (wk * v_ext).astype(BF16), (((0,), (0,)), ((), ())),
                              preferred_element_type=F32)
        s_ref[h] = decay * state + upd
        m_ref[h:h + 1, :] = jnp.broadcast_to(m_new, (1, LANES))


def _mlstm_kernel(qf, kf, vf, gf, qb, kb, vb, gb, bias, of, ob, sf, sb, mf, mb):
    c = pl.program_id(1)
    nc = pl.num_programs(1)

    @pl.when(c == 0)
    def _():
        sf[...] = jnp.zeros_like(sf)
        sb[...] = jnp.zeros_like(sb)
        mf[...] = jnp.zeros_like(mf)
        mb[...] = jnp.zeros_like(mb)

    _mlstm_direction(qf, kf, vf, gf, bias, of, sf, mf, c * BLOCK, True)
    _mlstm_direction(qb, kb, vb, gb, bias, ob, sb, mb, (nc - 1 - c) * BLOCK, False)


def _mlstm(proj, gates, bias, B, nc):
    T = proj.shape[0]

    def fwd_map(col):
        return lambda b, c: (b * nc + c, col)

    def bwd_map(col):
        return lambda b, c: (b * nc + nc - 1 - c, col)

    def specs(mk):
        return [pl.BlockSpec((BLOCK, M_QK), mk(OFF_MQ // M_QK)),
                pl.BlockSpec((BLOCK, M_QK), mk(OFF_MK // M_QK)),
                pl.BlockSpec((BLOCK, M_V), mk(OFF_MV // M_V))]

    gate_f = pl.BlockSpec((BLOCK, LANES), fwd_map(0))
    gate_b = pl.BlockSpec((BLOCK, LANES), bwd_map(0))
    out = jax.ShapeDtypeStruct((T, M_V), BF16)
    return pl.pallas_call(
        _mlstm_kernel,
        grid=(B, nc),
        in_specs=(specs(fwd_map) + [gate_f] + specs(bwd_map) + [gate_b]
                  + [pl.BlockSpec((1, LANES), lambda b, c: (0, 0))]),
        out_specs=[pl.BlockSpec((BLOCK, M_V), fwd_map(0)), pl.BlockSpec((BLOCK, M_V), bwd_map(0))],
        out_shape=[out, out],
        scratch_shapes=[pltpu.VMEM((M_HEADS, M_DQK, STATE_W), F32),
                        pltpu.VMEM((M_HEADS, M_DQK, STATE_W), F32),
                        pltpu.VMEM((M_HEADS, LANES), F32),
                        pltpu.VMEM((M_HEADS, LANES), F32)],
        compiler_params=_params(("parallel", "arbitrary")),
        name="mlstm",
    )(proj, proj, proj, gates, proj, proj, proj, gates, bias)


def _qk_prep_kernel(q_ref, k_ref, v_ref, cos_ref, sin_ref, gq_ref, gk_ref, qo_ref, ko_ref, vo_ref):
    cos = cos_ref[...]
    sin = sin_ref[...]

    def norm_rope(x, g):
        xn = x * lax.rsqrt(jnp.mean(x * x, axis=-1, keepdims=True) + EPS) * g
        return xn * cos + pltpu.roll(xn, A_DH // 2, axis=1) * sin

    for h in range(A_HEADS):
        sl = slice(h * A_DH, (h + 1) * A_DH)
        qo_ref[:, sl] = (norm_rope(q_ref[:, sl].astype(F32), gq_ref[...]) * (A_DH ** -0.5)).astype(BF16)
    for h in range(A_KV_HEADS):
        sl = slice(h * A_DH, (h + 1) * A_DH)
        ko_ref[:, sl] = norm_rope(k_ref[:, sl].astype(F32), gk_ref[...]).astype(BF16)
    vo_ref[...] = v_ref[...].astype(BF16)


def _qk_prep(proj, cos, sin, g_q, g_k):
    T = proj.shape[0]
    tm = _pick_tile(T, 512, 128)
    row = lambda col: (lambda i: (i, col))
    const = lambda i: (0, 0)
    return pl.pallas_call(
        _qk_prep_kernel,
        grid=(T // tm,),
        in_specs=[pl.BlockSpec((tm, A_Q), row(OFF_AQ // A_Q)),
                  pl.BlockSpec((tm, A_KV), row(OFF_AK // A_KV)),
                  pl.BlockSpec((tm, A_KV), row(OFF_AV // A_KV)),
                  pl.BlockSpec((tm, A_DH), row(0)),
                  pl.BlockSpec((tm, A_DH), row(0)),
                  pl.BlockSpec((1, A_DH), const),
                  pl.BlockSpec((1, A_DH), const)],
        out_specs=[pl.BlockSpec((tm, A_Q), row(0)),
                   pl.BlockSpec((tm, A_KV), row(0)),
                   pl.BlockSpec((tm, A_KV), row(0))],
        out_shape=[jax.ShapeDtypeStruct((T, A_Q), BF16),
                   jax.ShapeDtypeStruct((T, A_KV), BF16),
                   jax.ShapeDtypeStruct((T, A_KV), BF16)],
        compiler_params=_params(("parallel",)),
        name="qk_prep",
    )(proj, proj, proj, cos, sin, g_q, g_k)


def _attn_kernel(sink_ref, q_ref, kp, kc, kn, km, vp, vc, vn, vm, o_ref):
    n = pl.program_id(1)
    nb = pl.num_programs(1)
    R = A_GROUP * BLOCK
    W = 4 * BLOCK
    row = lax.broadcasted_iota(jnp.int32, (R, W), 0)
    col = lax.broadcasted_iota(jnp.int32, (R, W), 1)
    t = row % BLOCK
    seg = col // BLOCK
    off = col % BLOCK
    dist = (1 - seg) * BLOCK + (t - off)
    blk = n - 1 + seg
    win_ok = (seg < 3) & (jnp.abs(dist) <= WINDOW) & (blk >= 1) & (blk <= nb - 1)
    meta_ok = (seg == 3) & (off >= META_START)
    mask = win_ok | meta_ok
    head_row = lax.broadcasted_iota(jnp.int32, (R, 1), 0) // BLOCK
    for j in range(A_KV_HEADS):
        sl = slice(j * A_DH, (j + 1) * A_DH)
        q4 = jnp.concatenate([q_ref[:, (j * A_GROUP + g) * A_DH:(j * A_GROUP + g + 1) * A_DH]
                              for g in range(A_GROUP)], axis=0)
        kcat = jnp.concatenate([kp[:, sl], kc[:, sl], kn[:, sl], km[:, sl]], axis=0)
        vcat = jnp.concatenate([vp[:, sl], vc[:, sl], vn[:, sl], vm[:, sl]], axis=0)
        s = lax.dot_general(q4, kcat, (((1,), (1,)), ((), ())), preferred_element_type=F32)
        s = jnp.where(mask, s, -jnp.inf)
        sink = jnp.zeros((R, 1), F32)
        for g in range(A_GROUP):
            sink = jnp.where(head_row == g, sink_ref[j * A_GROUP + g], sink)
        m = jnp.maximum(jnp.max(s, axis=1, keepdims=True), sink)
        p = jnp.exp(s - m)
        denom = jnp.sum(p, axis=1, keepdims=True) + jnp.exp(sink - m)
        o = jnp.dot(p.astype(BF16), vcat, preferred_element_type=F32) / denom
        for g in range(A_GROUP):
            hh = j * A_GROUP + g
            o_ref[:, hh * A_DH:(hh + 1) * A_DH] = o[g * BLOCK:(g + 1) * BLOCK].astype(BF16)


def _attention(q, k, v, sink, B, nb):
    T = q.shape[0]
    cur = lambda b, n, s: (b * nb + n, 0)
    prev = lambda b, n, s: (b * nb + jnp.maximum(n - 1, 0), 0)
    nxt = lambda b, n, s: (b * nb + jnp.minimum(n + 1, nb - 1), 0)
    meta = lambda b, n, s: (b * nb, 0)
    kv = lambda m: pl.BlockSpec((BLOCK, A_KV), m)
    grid_spec = pltpu.PrefetchScalarGridSpec(
        num_scalar_prefetch=1,
        grid=(B, nb),
        in_specs=[pl.BlockSpec((BLOCK, A_Q), cur),
                  kv(prev), kv(cur), kv(nxt), kv(meta), kv(prev), kv(cur), kv(nxt), kv(meta)],
        out_specs=pl.BlockSpec((BLOCK, A_Q), cur))
    return pl.pallas_call(
        _attn_kernel,
        grid_spec=grid_spec,
        out_shape=jax.ShapeDtypeStruct((T, A_Q), BF16),
        compiler_params=_params(("parallel", "arbitrary")),
        name="band_attn",
    )(sink, q, k, k, k, k, v, v, v, v)


MERGE_TN = 512


def _merge_kernel(hf_ref, hb_ref, mo_ref, gh_ref, oa_ref, gm_ref, b_ref, wm_ref, wa_ref, z_ref, hm_ref):
    for h in range(M_HEADS):
        sl = slice(h * M_DV, (h + 1) * M_DV)
        x = hf_ref[:, sl].astype(F32) + hb_ref[:, sl].astype(F32)
        xn = x * lax.rsqrt(jnp.mean(x * x, axis=-1, keepdims=True) + EPS) * gh_ref[:, sl]
        hm_ref[:, sl] = (xn * jax.nn.sigmoid(mo_ref[:, sl].astype(F32))).astype(BF16)
    for j in range(D_MODEL // MERGE_TN):
        cm = slice(j * MERGE_TN, (j + 1) * MERGE_TN)
        ca = slice(D_MODEL + j * MERGE_TN, D_MODEL + (j + 1) * MERGE_TN)
        y_m = jnp.dot(hm_ref[...], wm_ref[:, cm], preferred_element_type=F32)
        y_a = jnp.dot(oa_ref[...], wa_ref[:, cm], preferred_element_type=F32)
        gate_m = jax.nn.sigmoid(gm_ref[:, cm].astype(F32) + b_ref[:, cm])
        gate_a = jax.nn.sigmoid(gm_ref[:, ca].astype(F32) + b_ref[:, ca])
        z_ref[:, cm] = (gate_m * y_m + gate_a * y_a).astype(BF16)


def _merge(hf, hb, proj, g_mhead, oa, b_merge, wm, wa):
    T = hf.shape[0]
    tm = _pick_tile(T, 512, 128)
    row = lambda col: (lambda i: (i, col))
    const = lambda i: (0, 0)
    once = pl.Buffered(1)
    return pl.pallas_call(
        _merge_kernel,
        grid=(T // tm,),
        in_specs=[pl.BlockSpec((tm, M_V), row(0)),
                  pl.BlockSpec((tm, M_V), row(0)),
                  pl.BlockSpec((tm, M_V), row(OFF_MO // M_V)),
                  pl.BlockSpec((1, M_V), const),
                  pl.BlockSpec((tm, A_Q), row(0)),
                  pl.BlockSpec((tm, 2 * D_MODEL), row(OFF_GM // (2 * D_MODEL))),
                  pl.BlockSpec((1, 2 * D_MODEL), const),
                  pl.BlockSpec((M_V, D_MODEL), const, pipeline_mode=once),
                  pl.BlockSpec((A_Q, D_MODEL), const, pipeline_mode=once)],
        out_specs=pl.BlockSpec((tm, D_MODEL), row(0)),
        out_shape=jax.ShapeDtypeStruct((T, D_MODEL), BF16),
        scratch_shapes=[pltpu.VMEM((tm, M_V), BF16)],
        compiler_params=_params(("parallel",)),
        name="merge",
    )(hf, hb, proj, g_mhead, oa, proj, b_merge, wm, wa)


def _out_proj_kernel(z_ref, w_ref, h_ref, o_ref):
    o_ref[...] = h_ref[...] + jnp.dot(z_ref[...], w_ref[...], preferred_element_type=F32)


def _out_proj(z, w, h):
    T = z.shape[0]
    tm = _pick_tile(T, 1024, 256)
    tn = 512
    return pl.pallas_call(
        _out_proj_kernel,
        grid=(T // tm, D_MODEL // tn),
        in_specs=[pl.BlockSpec((tm, D_MODEL), lambda i, j: (i, 0)),
                  pl.BlockSpec((D_MODEL, tn), lambda i, j: (0, j)),
                  pl.BlockSpec((tm, tn), lambda i, j: (i, j))],
        out_specs=pl.BlockSpec((tm, tn), lambda i, j: (i, j)),
        out_shape=jax.ShapeDtypeStruct((T, D_MODEL), F32),
        compiler_params=_params(("parallel", "arbitrary")),
        name="out_proj",
    )(z, w, h)


def _top_ranked(s, count):
    vals = []
    rank = jnp.full(s.shape, float(count), F32)
    for i in range(count):
        m = jnp.max(s, axis=0, keepdims=True)
        vals.append(m)
        hit = s >= m
        rank = jnp.where(hit, float(i), rank)
        s = jnp.where(hit, -jnp.inf, s)
    return vals, rank


def _peer_score_kernel(h_ref, g_ref, wq_ref, k1_ref, k2_ref, xt_ref, n_ref, c_ref, r2_ref, e2_ref):
    x = h_ref[...]
    x = x * lax.rsqrt(jnp.mean(x * x, axis=-1, keepdims=True) + EPS) * g_ref[...]
    xt = x.T.astype(BF16)
    xt_ref[...] = xt
    qt = jnp.dot(wq_ref[...], xt, preferred_element_type=F32)
    half = P_DQ // 2
    for h in range(P_HEADS):
        q1 = qt[h * P_DQ:h * P_DQ + half].astype(BF16)
        q2 = qt[h * P_DQ + half:(h + 1) * P_DQ].astype(BF16)
        s1 = jnp.dot(k1_ref[h], q1, preferred_element_type=F32)
        s2 = jnp.dot(k2_ref[h], q2, preferred_element_type=F32)
        v1, r1 = _top_ranked(s1, P_TOPK)
        v2, r2 = _top_ranked(s2, P_TOPK)
        rows = [[v1[i] + v2[j] for j in range(P_TOPK // (i + 1))] for i in range(P_TOPK)]
        cand = jnp.concatenate([c for row in rows for c in row], axis=0)
        s = cand
        for _ in range(P_TOPK):
            c16 = jnp.max(s, axis=0, keepdims=True)
            s = jnp.where(s >= c16, -jnp.inf, s)
        cmax = v1[0] + v2[0]
        z = jnp.sum(jnp.where(cand >= c16, jnp.exp(cand - cmax), 0.0), axis=0, keepdims=True)
        n_sel = jnp.zeros_like(s1)
        for i in range(P_TOPK):
            cnt = sum((c >= c16).astype(F32) for c in rows[i])
            n_sel = jnp.where(r1 == float(i), cnt, n_sel)
        sl = slice(h * P_NKEYS, (h + 1) * P_NKEYS)
        n_ref[sl, :] = n_sel
        c_ref[sl, :] = jnp.exp(s1 - v1[0]) / z
        r2_ref[sl, :] = r2.astype(BF16)
        e2_ref[sl, :] = jnp.exp(s2 - v2[0]).astype(BF16)


def _peer_scores(h, g, wq_t, k1, k2):
    T = h.shape[0]
    tm = _pick_tile(T, 256, 128)
    HK = P_HEADS * P_NKEYS
    tok = lambda i: (0, i)
    f32_side = jax.ShapeDtypeStruct((HK, T), F32)
    bf16_side = jax.ShapeDtypeStruct((HK, T), BF16)
    return pl.pallas_call(
        _peer_score_kernel,
        grid=(T // tm,),
        in_specs=[pl.BlockSpec((tm, D_MODEL), lambda i: (i, 0)),
                  pl.BlockSpec((1, D_MODEL), lambda i: (0, 0)),
                  pl.BlockSpec((P_HEADS * P_DQ, D_MODEL), lambda i: (0, 0)),
                  pl.BlockSpec((P_HEADS, P_NKEYS, P_DQ // 2), lambda i: (0, 0, 0)),
                  pl.BlockSpec((P_HEADS, P_NKEYS, P_DQ // 2), lambda i: (0, 0, 0))],
        out_specs=[pl.BlockSpec((D_MODEL, tm), tok)] + [pl.BlockSpec((HK, tm), tok)] * 4,
        out_shape=[jax.ShapeDtypeStruct((D_MODEL, T), BF16), f32_side, f32_side, bf16_side, bf16_side],
        compiler_params=_params(("parallel",)),
        name="peer_scores",
    )(h, g, wq_t, k1, k2)


PEER_TE = 1024
PEER_NJ = P_NEXP // PEER_TE
PEER_DROWS = 512


def _peer_expert_kernel(xt_ref, n_ref, c_ref, r2_ref, e2_ref, u_ref, vt_ref, o_ref, w0_ref, w1_ref, a_ref):
    j = pl.program_id(1)
    nj = pl.num_programs(1) - 1
    na = PEER_TE // P_NKEYS
    w_ref = (w0_ref, w1_ref)

    def gate_block(a_t, slot, al):
        a = j * na + al
        gate = None
        for h in range(P_HEADS):
            n_sel = n_ref[pl.ds(h * P_NKEYS + a, 1), :].astype(BF16)
            coef = c_ref[pl.ds(h * P_NKEYS + a, 1), :].astype(BF16)
            sl = slice(h * P_NKEYS, (h + 1) * P_NKEYS)
            term = jnp.where(r2_ref[sl, :] < n_sel, e2_ref[sl, :], 0.0) * coef
            gate = term if gate is None else gate + term
        x = a_t[al * P_NKEYS:(al + 1) * P_NKEYS]
        act = 0.5 * x * (1.0 + lax.erf(x * (2.0 ** -0.5)))
        w_ref[slot][al * P_NKEYS:(al + 1) * P_NKEYS, :] = gate * act.astype(BF16)

    def accumulate(slot, k):
        rows = slice(k * PEER_DROWS, (k + 1) * PEER_DROWS)
        o_ref[rows, :] += jnp.dot(vt_ref[rows, :], w_ref[slot][...], preferred_element_type=F32)

    n_acc = D_MODEL // PEER_DROWS

    @pl.when(j == 0)
    def _():
        o_ref[...] = jnp.zeros_like(o_ref)
        w1_ref[...] = jnp.zeros_like(w1_ref)

    for parity in range(2):
        @pl.when((j < nj) & (j % 2 == parity))
        def _():
            half = PEER_TE // 2
            for r in range(2):
                rows = slice(r * half, (r + 1) * half)
                a_ref[rows, :] = jnp.dot(u_ref[rows, :], xt_ref[...], preferred_element_type=F32)
            for r in range(2):
                for al in range(r * na // 2, (r + 1) * na // 2):
                    gate_block(a_ref, parity, al)
                for k in range(r * n_acc // 2, (r + 1) * n_acc // 2):
                    accumulate(1 - parity, k)

    @pl.when(j == nj)
    def _():
        for k in range(n_acc):
            accumulate((PEER_NJ - 1) % 2, k)


def _peer_experts(xt, n_sel, coef, r2, e2, u, vt):
    T = xt.shape[1]
    tm = _pick_tile(T, 768, 256)
    te = PEER_TE
    nj = PEER_NJ
    HK = P_HEADS * P_NKEYS
    once = pl.Buffered(1)
    tok = lambda i, j: (0, i)
    return pl.pallas_call(
        _peer_expert_kernel,
        grid=(T // tm, nj + 1),
        in_specs=[pl.BlockSpec((D_MODEL, tm), tok, pipeline_mode=once)]
                 + [pl.BlockSpec((HK, tm), tok, pipeline_mode=once)] * 4
                 + [pl.BlockSpec((te, D_MODEL), lambda i, j: (jnp.minimum(j, nj - 1), 0)),
                    pl.BlockSpec((D_MODEL, te), lambda i, j: (0, jnp.maximum(j - 1, 0)))],
        out_specs=pl.BlockSpec((D_MODEL, tm), tok),
        out_shape=jax.ShapeDtypeStruct((D_MODEL, T), F32),
        scratch_shapes=[pltpu.VMEM((te, tm), BF16), pltpu.VMEM((te, tm), BF16), pltpu.VMEM((te, tm), F32)],
        compiler_params=_params(("parallel", "arbitrary")),
        name="peer_experts",
    )(xt, n_sel, coef, r2, e2, u, vt)


def _add_transposed_kernel(h_ref, yt_ref, o_ref):
    o_ref[...] = h_ref[...] + yt_ref[...].T


def _add_transposed(h, yt):
    T, D = h.shape
    tm = _pick_tile(T, 512, 128)
    return pl.pallas_call(
        _add_transposed_kernel,
        grid=(T // tm,),
        in_specs=[pl.BlockSpec((tm, D), lambda i: (i, 0)), pl.BlockSpec((D, tm), lambda i: (0, i))],
        out_specs=pl.BlockSpec((tm, D), lambda i: (i, 0)),
        out_shape=jax.ShapeDtypeStruct((T, D), F32),
        compiler_params=_params(("parallel",)),
        name="add_transposed",
    )(h, yt)


def _prep_layer(l, g_mix, w_in, b_mgate, g_mhead, g_q, g_k, attn_sink, w_branch_m, w_branch_a,
                b_merge, w_out, g_ffn, w_pq, sub_k1, sub_k2, u_tab, v_tab):
    n_gate0 = 2 * M_QK + 2 * M_V
    n_gm0 = n_gate0 + N_GATE + A_Q + 2 * A_KV
    w = w_in[l]
    w_main = jnp.concatenate([w[:, n_gm0:], w[:, :n_gate0], w[:, n_gate0 + N_GATE:n_gm0]], axis=1).astype(BF16)
    w_gate = jnp.pad(w[:, n_gate0:n_gate0 + N_GATE], ((0, 0), (0, LANES - N_GATE))).astype(BF16)
    bias = jnp.pad(b_mgate[l], (0, LANES - N_GATE)).reshape(1, LANES)
    return dict(
        g_mix=g_mix[l].reshape(1, D_MODEL), w_main=w_main, w_gate=w_gate, bias=bias,
        g_mhead=g_mhead[l].reshape(1, M_V), g_q=g_q[l].reshape(1, A_DH), g_k=g_k[l].reshape(1, A_DH),
        sink=attn_sink[l], wm=w_branch_m[l].astype(BF16), wa=w_branch_a[l].astype(BF16),
        b_merge=b_merge[l].reshape(1, 2 * D_MODEL), wo=w_out[l].astype(BF16),
        g_ffn=g_ffn[l].reshape(1, D_MODEL), wq_t=w_pq[l].T.astype(BF16),
        k1=sub_k1[l].astype(BF16), k2=sub_k2[l].astype(BF16),
        u=u_tab[l].astype(BF16), vt=v_tab[l].T.astype(BF16))


def _layer(h, p, cos, sin, B, nc):
    proj = _norm_proj(h, p["g_mix"], p["w_main"], 1024, BF16)
    gates = _norm_proj(h, p["g_mix"], p["w_gate"], LANES, F32)
    hf, hb = _mlstm(proj, gates, p["bias"], B, nc)
    q, k, v = _qk_prep(proj, cos, sin, p["g_q"], p["g_k"])
    oa = _attention(q, k, v, p["sink"], B, nc)
    z = _merge(hf, hb, proj, p["g_mhead"], oa, p["b_merge"], p["wm"], p["wa"])
    h = _out_proj(z, p["wo"], h)
    xt, n_sel, coef, r2, e2 = _peer_scores(h, p["g_ffn"], p["wq_t"], p["k1"], p["k2"])
    return _add_transposed(h, _peer_experts(xt, n_sel, coef, r2, e2, p["u"], p["vt"]))


def _trunk(x, meta_tokens, layers):
    B, L, D = x.shape
    Lp = BLOCK + L
    nc = Lp // BLOCK
    meta = jnp.broadcast_to(meta_tokens.astype(x.dtype)[None], (B, N_META, D))
    h = jnp.concatenate([jnp.zeros((B, META_START, D), x.dtype), meta, x], axis=1).reshape(B * Lp, D)
    pos = (jnp.arange(Lp) - META_START).astype(F32)
    half = A_DH // 2
    inv = ROPE_THETA ** (-jnp.arange(half, dtype=F32) / half)
    ang = pos[:, None] * inv[None, :]
    cos = jnp.tile(jnp.concatenate([jnp.cos(ang), jnp.cos(ang)], axis=1), (B, 1))
    sin = jnp.tile(jnp.concatenate([-jnp.sin(ang), jnp.sin(ang)], axis=1), (B, 1))
    for p in layers:
        h = _layer(h, p, cos, sin, B, nc)
    return h.reshape(B, Lp, D)[:, BLOCK:]


def kernel(x_prompt, x_sample, meta_tokens, g_mix, w_in, b_mgate, g_mhead, g_q, g_k, attn_sink,
           w_branch_m, w_branch_a, b_merge, w_out, g_ffn, w_pq, sub_k1, sub_k2, u_tab, v_tab):
    weights = (g_mix, w_in, b_mgate, g_mhead, g_q, g_k, attn_sink, w_branch_m, w_branch_a,
               b_merge, w_out, g_ffn, w_pq, sub_k1, sub_k2, u_tab, v_tab)
    layers = [_prep_layer(l, *weights) for l in range(w_in.shape[0])]
    return (_trunk(x_prompt, meta_tokens, layers), _trunk(x_sample, meta_tokens, layers))
```

```python
import functools

import jax
import jax.numpy as jnp
from jax import lax
from jax.experimental import pallas as pl
from jax.experimental.pallas import tpu as pltpu

F32 = jnp.float32
BF16 = jnp.bfloat16

D_MODEL = 2048
DEPTH = 4
N_META = 16
BLOCK = 128
META_START = BLOCK - N_META
EPS = 1e-6
M_HEADS = 8
M_DQK = 128
M_DV = 256
A_HEADS = 16
A_KV_HEADS = 4
A_GROUP = A_HEADS // A_KV_HEADS
A_DH = 128
WINDOW = 128
ROPE_THETA = 10000.0
P_HEADS = 8
P_NKEYS = 128
P_NEXP = P_NKEYS * P_NKEYS
P_DQ = 256
P_TOPK = 16

M_QK = M_HEADS * M_DQK
M_V = M_HEADS * M_DV
A_Q = A_HEADS * A_DH
A_KV = A_KV_HEADS * A_DH
OFF_GM = 0
OFF_MQ = OFF_GM + 2 * D_MODEL
OFF_MK = OFF_MQ + M_QK
OFF_MV = OFF_MK + M_QK
OFF_MO = OFF_MV + M_V
OFF_AQ = OFF_MO + M_V
OFF_AK = OFF_AQ + A_Q
OFF_AV = OFF_AK + A_KV
N_MAIN = OFF_AV + A_KV
N_GATE = 4 * M_HEADS
LANES = 128
STATE_W = M_DV + LANES
VMEM_LIMIT = 56 * 1024 * 1024


def _pick_tile(n, cap, mult):
    best = None
    for t in range(mult, min(n, cap) + 1, mult):
        if n % t == 0:
            best = t
    assert best is not None, (n, cap, mult)
    return best


def _params(sem):
    return pltpu.CompilerParams(dimension_semantics=sem, vmem_limit_bytes=VMEM_LIMIT)


def _norm_proj_kernel(h_ref, g_ref, w_ref, o_ref, xn_ref):
    @pl.when(pl.program_id(1) == 0)
    def _():
        x = h_ref[...]
        ms = jnp.mean(x * x, axis=-1, keepdims=True)
        xn_ref[...] = (x * lax.rsqrt(ms + EPS) * g_ref[...]).astype(BF16)

    o_ref[...] = jnp.dot(xn_ref[...], w_ref[...], preferred_element_type=F32).astype(o_ref.dtype)


def _column_tiles(w, tn):
    K, N = w.shape
    return w.reshape(K, N // tn, tn).transpose(1, 0, 2)


def _norm_proj(h, g, w_tiles, out_dtype):
    T, D = h.shape
    nj, _, tn = w_tiles.shape
    tm = _pick_tile(T, 1024, 256)
    return pl.pallas_call(
        _norm_proj_kernel,
        grid=(T // tm, nj),
        in_specs=[pl.BlockSpec((tm, D), lambda i, j: (i, 0)),
                  pl.BlockSpec((1, D), lambda i, j: (0, 0)),
                  pl.BlockSpec((None, D, tn), lambda i, j: (j, 0, 0))],
        out_specs=pl.BlockSpec((tm, tn), lambda i, j: (i, j)),
        out_shape=jax.ShapeDtypeStruct((T, nj * tn), out_dtype),
        scratch_shapes=[pltpu.VMEM((tm, D), BF16)],
        compiler_params=_params(("parallel", "arbitrary")),
        name="norm_proj",
    )(h, g, w_tiles)


def _mlstm_direction(q_ref, k_ref, v_ref, g_ref, bias_ref, o_ref, s_ref, m_ref, tok0, fwd):
    T = BLOCK
    g = g_ref[...] + bias_ref[...]
    rows = lax.broadcasted_iota(jnp.int32, (T, LANES), 0)
    valid = (tok0 + rows) >= META_START
    logsig = jnp.minimum(g, 0.0) - jnp.log1p(jnp.exp(-jnp.abs(g)))
    lf = jnp.where(valid, logsig, 0.0)
    li = jnp.where(valid, g, -jnp.inf)
    r = lax.broadcasted_iota(jnp.int32, (T, T), 0)
    c = lax.broadcasted_iota(jnp.int32, (T, T), 1)
    tri = (c <= r) if fwd else (c >= r)
    bcum = jnp.dot(tri.astype(F32), lf, preferred_element_type=F32,
                   precision=lax.Precision.HIGHEST)
    bcum_t = bcum.T
    li_t = li.T
    i_off = 0 if fwd else 2 * M_HEADS
    f_off = i_off + M_HEADS
    last = T - 1 if fwd else 0
    ones = jnp.ones((T, LANES), BF16)
    for h in range(M_HEADS):
        b_col = bcum[:, f_off + h:f_off + h + 1]
        b_row = bcum_t[f_off + h:f_off + h + 1, :]
        li_col = li[:, i_off + h:i_off + h + 1]
        li_row = li_t[i_off + h:i_off + h + 1, :]
        b_last = bcum[last:last + 1, f_off + h:f_off + h + 1]
        m_old = m_ref[h:h + 1, 0:1]
        qb = q_ref[:, h * M_DQK:(h + 1) * M_DQK].astype(BF16)
        kb = (k_ref[:, h * M_DQK:(h + 1) * M_DQK] * (M_DQK ** -0.5)).astype(BF16)
        v_ext = jnp.concatenate([v_ref[:, h * M_DV:(h + 1) * M_DV], ones], axis=1)
        state = s_ref[h]

        dm = jnp.where(tri, b_col - b_row + li_row, -jnp.inf)
        inter = b_col + m_old
        m_t = jnp.maximum(inter, jnp.max(dm, axis=1, keepdims=True))
        s_qk = lax.dot_general(qb, kb, (((1,), (1,)), ((), ())), preferred_element_type=F32)
        p = (jnp.exp(dm - m_t) * s_qk).astype(BF16)
        w_inter = jnp.exp(inter - m_t)
        nd = (jnp.dot(p, v_ext.astype(BF16), preferred_element_type=F32)
              + w_inter * jnp.dot(qb, state.astype(BF16), preferred_element_type=F32))
        den = jnp.maximum(jnp.abs(nd[:, M_DV:]), jnp.exp(-m_t))
        inv = 1.0 / den
        o_ref[:, h * M_DV:(h + 1) * M_DV] = (
            nd[:, :M_DV] * jnp.concatenate([inv, inv], axis=1)).astype(o_ref.dtype)

        gk = b_last - b_col + li_col
        m_new = jnp.maximum(b_last + m_old, jnp.max(gk, axis=0, keepdims=True))
        wk = jnp.exp(gk - m_new)
        decay = jnp.exp(b_last + m_old - m_new)
        upd = lax.dot_general(kb, (wk * v_ext).astype(BF16), (((0,), (0,)), ((), ())),
                              preferred_element_type=F32)
        s_ref[h] = decay * state + upd
        m_ref[h:h + 1, :] = jnp.broadcast_to(m_new, (1, LANES))


def _mlstm_kernel(qf, kf, vf, gf, qb, kb, vb, gb, bias, of, ob, sf, sb, mf, mb):
    c = pl.program_id(1)
    nc = pl.num_programs(1)

    @pl.when(c == 0)
    def _():
        sf[...] = jnp.zeros_like(sf)
        sb[...] = jnp.zeros_like(sb)
        mf[...] = jnp.zeros_like(mf)
        mb[...] = jnp.zeros_like(mb)

    _mlstm_direction(qf, kf, vf, gf, bias, of, sf, mf, c * BLOCK, True)
    _mlstm_direction(qb, kb, vb, gb, bias, ob, sb, mb, (nc - 1 - c) * BLOCK, False)


def _mlstm(proj, gates, bias, B, nc):
    T = proj.shape[0]

    def fwd_map(col):
        return lambda b, c: (b * nc + c, col)

    def bwd_map(col):
        return lambda b, c: (b * nc + nc - 1 - c, col)

    def specs(mk):
        return [pl.BlockSpec((BLOCK, M_QK), mk(OFF_MQ // M_QK)),
                pl.BlockSpec((BLOCK, M_QK), mk(OFF_MK // M_QK)),
                pl.BlockSpec((BLOCK, M_V), mk(OFF_MV // M_V))]

    gate_f = pl.BlockSpec((BLOCK, LANES), fwd_map(0))
    gate_b = pl.BlockSpec((BLOCK, LANES), bwd_map(0))
    out = jax.ShapeDtypeStruct((T, M_V), BF16)
    return pl.pallas_call(
        _mlstm_kernel,
        grid=(B, nc),
        in_specs=(specs(fwd_map) + [gate_f] + specs(bwd_map) + [gate_b]
                  + [pl.BlockSpec((1, LANES), lambda b, c: (0, 0))]),
        out_specs=[pl.BlockSpec((BLOCK, M_V), fwd_map(0)), pl.BlockSpec((BLOCK, M_V), bwd_map(0))],
        out_shape=[out, out],
        scratch_shapes=[pltpu.VMEM((M_HEADS, M_DQK, STATE_W), F32),
                        pltpu.VMEM((M_HEADS, M_DQK, STATE_W), F32),
                        pltpu.VMEM((M_HEADS, LANES), F32),
                        pltpu.VMEM((M_HEADS, LANES), F32)],
        compiler_params=_params(("parallel", "arbitrary")),
        name="mlstm",
    )(proj, proj, proj, gates, proj, proj, proj, gates, bias)


def _qk_prep_kernel(q_ref, k_ref, v_ref, cos_ref, sin_ref, gq_ref, gk_ref, qo_ref, ko_ref, vo_ref):
    cos = cos_ref[...]
    sin = sin_ref[...]

    def norm_rope(x, g):
        xn = x * lax.rsqrt(jnp.mean(x * x, axis=-1, keepdims=True) + EPS) * g
        return xn * cos + pltpu.roll(xn, A_DH // 2, axis=1) * sin

    for h in range(A_HEADS):
        sl = slice(h * A_DH, (h + 1) * A_DH)
        qo_ref[:, sl] = (norm_rope(q_ref[:, sl].astype(F32), gq_ref[...]) * (A_DH ** -0.5)).astype(BF16)
    for h in range(A_KV_HEADS):
        sl = slice(h * A_DH, (h + 1) * A_DH)
        ko_ref[:, sl] = norm_rope(k_ref[:, sl].astype(F32), gk_ref[...]).astype(BF16)
    vo_ref[...] = v_ref[...].astype(BF16)


def _qk_prep(proj, cos, sin, g_q, g_k):
    T = proj.shape[0]
    tm = _pick_tile(T, 512, 128)
    row = lambda col: (lambda i: (i, col))
    const = lambda i: (0, 0)
    return pl.pallas_call(
        _qk_prep_kernel,
        grid=(T // tm,),
        in_specs=[pl.BlockSpec((tm, A_Q), row(OFF_AQ // A_Q)),
                  pl.BlockSpec((tm, A_KV), row(OFF_AK // A_KV)),
                  pl.BlockSpec((tm, A_KV), row(OFF_AV // A_KV)),
                  pl.BlockSpec((tm, A_DH), row(0)),
                  pl.BlockSpec((tm, A_DH), row(0)),
                  pl.BlockSpec((1, A_DH), const),
                  pl.BlockSpec((1, A_DH), const)],
        out_specs=[pl.BlockSpec((tm, A_Q), row(0)),
                   pl.BlockSpec((tm, A_KV), row(0)),
                   pl.BlockSpec((tm, A_KV), row(0))],
        out_shape=[jax.ShapeDtypeStruct((T, A_Q), BF16),
                   jax.ShapeDtypeStruct((T, A_KV), BF16),
                   jax.ShapeDtypeStruct((T, A_KV), BF16)],
        compiler_params=_params(("parallel",)),
        name="qk_prep",
    )(proj, proj, proj, cos, sin, g_q, g_k)


def _attn_kernel(sink_ref, q_ref, kp, kc, kn, km, vp, vc, vn, vm, o_ref):
    n = pl.program_id(1)
    nb = pl.num_programs(1)
    R = A_GROUP * BLOCK
    W = 4 * BLOCK
    row = lax.broadcasted_iota(jnp.int32, (R, W), 0)
    col = lax.broadcasted_iota(jnp.int32, (R, W), 1)
    t = row % BLOCK
    seg = col // BLOCK
    off = col % BLOCK
    dist = (1 - seg) * BLOCK + (t - off)
    blk = n - 1 + seg
    win_ok = (seg < 3) & (jnp.abs(dist) <= WINDOW) & (blk >= 1) & (blk <= nb - 1)
    meta_ok = (seg == 3) & (off >= META_START)
    mask = win_ok | meta_ok
    head_row = lax.broadcasted_iota(jnp.int32, (R, 1), 0) // BLOCK
    for j in range(A_KV_HEADS):
        sl = slice(j * A_DH, (j + 1) * A_DH)
        q4 = jnp.concatenate([q_ref[:, (j * A_GROUP + g) * A_DH:(j * A_GROUP + g + 1) * A_DH]
                              for g in range(A_GROUP)], axis=0)
        kcat = jnp.concatenate([kp[:, sl], kc[:, sl], kn[:, sl], km[:, sl]], axis=0)
        vcat = jnp.concatenate([vp[:, sl], vc[:, sl], vn[:, sl], vm[:, sl]], axis=0)
        s = lax.dot_general(q4, kcat, (((1,), (1,)), ((), ())), preferred_element_type=F32)
        s = jnp.where(mask, s, -jnp.inf)
        sink = jnp.zeros((R, 1), F32)
        for g in range(A_GROUP):
            sink = jnp.where(head_row == g, sink_ref[j * A_GROUP + g], sink)
        m = jnp.maximum(jnp.max(s, axis=1, keepdims=True), sink)
        p = jnp.exp(s - m)
        denom = jnp.sum(p, axis=1, keepdims=True) + jnp.exp(sink - m)
        o = jnp.dot(p.astype(BF16), vcat, preferred_element_type=F32) / denom
        for g in range(A_GROUP):
            hh = j * A_GROUP + g
            o_ref[:, hh * A_DH:(hh + 1) * A_DH] = o[g * BLOCK:(g + 1) * BLOCK].astype(BF16)


def _attention(q, k, v, sink, B, nb):
    T = q.shape[0]
    cur = lambda b, n, s: (b * nb + n, 0)
    prev = lambda b, n, s: (b * nb + jnp.maximum(n - 1, 0), 0)
    nxt = lambda b, n, s: (b * nb + jnp.minimum(n + 1, nb - 1), 0)
    meta = lambda b, n, s: (b * nb, 0)
    kv = lambda m: pl.BlockSpec((BLOCK, A_KV), m)
    grid_spec = pltpu.PrefetchScalarGridSpec(
        num_scalar_prefetch=1,
        grid=(B, nb),
        in_specs=[pl.BlockSpec((BLOCK, A_Q), cur),
                  kv(prev), kv(cur), kv(nxt), kv(meta), kv(prev), kv(cur), kv(nxt), kv(meta)],
        out_specs=pl.BlockSpec((BLOCK, A_Q), cur))
    return pl.pallas_call(
        _attn_kernel,
        grid_spec=grid_spec,
        out_shape=jax.ShapeDtypeStruct((T, A_Q), BF16),
        compiler_params=_params(("parallel", "arbitrary")),
        name="band_attn",
    )(sink, q, k, k, k, k, v, v, v, v)


MERGE_TN = 512


def _merge_kernel(hf_ref, hb_ref, mo_ref, gh_ref, oa_ref, gm_ref, b_ref, wm_ref, wa_ref, z_ref, hm_ref):
    for h in range(M_HEADS):
        sl = slice(h * M_DV, (h + 1) * M_DV)
        x = hf_ref[:, sl].astype(F32) + hb_ref[:, sl].astype(F32)
        xn = x * lax.rsqrt(jnp.mean(x * x, axis=-1, keepdims=True) + EPS) * gh_ref[:, sl]
        hm_ref[:, sl] = (xn * jax.nn.sigmoid(mo_ref[:, sl].astype(F32))).astype(BF16)
    for j in range(D_MODEL // MERGE_TN):
        cm = slice(j * MERGE_TN, (j + 1) * MERGE_TN)
        ca = slice(D_MODEL + j * MERGE_TN, D_MODEL + (j + 1) * MERGE_TN)
        y_m = jnp.dot(hm_ref[...], wm_ref[:, cm], preferred_element_type=F32)
        y_a = jnp.dot(oa_ref[...], wa_ref[:, cm], preferred_element_type=F32)
        gate_m = jax.nn.sigmoid(gm_ref[:, cm].astype(F32) + b_ref[:, cm])
        gate_a = jax.nn.sigmoid(gm_ref[:, ca].astype(F32) + b_ref[:, ca])
        z_ref[:, cm] = (gate_m * y_m + gate_a * y_a).astype(BF16)


def _merge(hf, hb, proj, g_mhead, oa, b_merge, wm, wa):
    T = hf.shape[0]
    tm = _pick_tile(T, 512, 128)
    row = lambda col: (lambda i: (i, col))
    const = lambda i: (0, 0)
    once = pl.Buffered(1)
    return pl.pallas_call(
        _merge_kernel,
        grid=(T // tm,),
        in_specs=[pl.BlockSpec((tm, M_V), row(0)),
                  pl.BlockSpec((tm, M_V), row(0)),
                  pl.BlockSpec((tm, M_V), row(OFF_MO // M_V)),
                  pl.BlockSpec((1, M_V), const),
                  pl.BlockSpec((tm, A_Q), row(0)),
                  pl.BlockSpec((tm, 2 * D_MODEL), row(OFF_GM // (2 * D_MODEL))),
                  pl.BlockSpec((1, 2 * D_MODEL), const),
                  pl.BlockSpec((M_V, D_MODEL), const, pipeline_mode=once),
                  pl.BlockSpec((A_Q, D_MODEL), const, pipeline_mode=once)],
        out_specs=pl.BlockSpec((tm, D_MODEL), row(0)),
        out_shape=jax.ShapeDtypeStruct((T, D_MODEL), BF16),
        scratch_shapes=[pltpu.VMEM((tm, M_V), BF16)],
        compiler_params=_params(("parallel",)),
        name="merge",
    )(hf, hb, proj, g_mhead, oa, proj, b_merge, wm, wa)


def _out_proj_kernel(z_ref, w_ref, h_ref, o_ref):
    for j in range(D_MODEL // MERGE_TN):
        cs = slice(j * MERGE_TN, (j + 1) * MERGE_TN)
        o_ref[:, cs] = h_ref[:, cs] + jnp.dot(z_ref[...], w_ref[:, cs], preferred_element_type=F32)


def _out_proj(z, w, h):
    T = z.shape[0]
    tm = _pick_tile(T, 1024, 256)
    row = lambda i: (i, 0)
    return pl.pallas_call(
        _out_proj_kernel,
        grid=(T // tm,),
        in_specs=[pl.BlockSpec((tm, D_MODEL), row),
                  pl.BlockSpec((D_MODEL, D_MODEL), lambda i: (0, 0), pipeline_mode=pl.Buffered(1)),
                  pl.BlockSpec((tm, D_MODEL), row)],
        out_specs=pl.BlockSpec((tm, D_MODEL), row),
        out_shape=jax.ShapeDtypeStruct((T, D_MODEL), F32),
        compiler_params=_params(("parallel",)),
        name="out_proj",
    )(z, w, h)


def _top_ranked(s, count):
    vals = []
    rank = jnp.full(s.shape, float(count), F32)
    for i in range(count):
        m = jnp.max(s, axis=0, keepdims=True)
        vals.append(m)
        hit = s >= m
        rank = jnp.where(hit, float(i), rank)
        s = jnp.where(hit, -jnp.inf, s)
    return vals, rank


def _peer_score_kernel(h_ref, g_ref, wq_ref, k1_ref, k2_ref, xt_ref, n_ref, c_ref, r2_ref, e2_ref):
    x = h_ref[...]
    x = x * lax.rsqrt(jnp.mean(x * x, axis=-1, keepdims=True) + EPS) * g_ref[...]
    xt = x.T.astype(BF16)
    xt_ref[...] = xt
    qt = jnp.dot(wq_ref[...], xt, preferred_element_type=F32)
    half = P_DQ // 2
    for h in range(P_HEADS):
        q1 = qt[h * P_DQ:h * P_DQ + half].astype(BF16)
        q2 = qt[h * P_DQ + half:(h + 1) * P_DQ].astype(BF16)
        s1 = jnp.dot(k1_ref[h], q1, preferred_element_type=F32)
        s2 = jnp.dot(k2_ref[h], q2, preferred_element_type=F32)
        v1, r1 = _top_ranked(s1, P_TOPK)
        v2, r2 = _top_ranked(s2, P_TOPK)
        rows = [[v1[i] + v2[j] for j in range(P_TOPK // (i + 1))] for i in range(P_TOPK)]
        cand = jnp.concatenate([c for row in rows for c in row], axis=0)
        s = cand
        for _ in range(P_TOPK):
            c16 = jnp.max(s, axis=0, keepdims=True)
            s = jnp.where(s >= c16, -jnp.inf, s)
        cmax = v1[0] + v2[0]
        z = jnp.sum(jnp.where(cand >= c16, jnp.exp(cand - cmax), 0.0), axis=0, keepdims=True)
        n_sel = jnp.zeros_like(s1)
        for i in range(P_TOPK):
            cnt = sum((c >= c16).astype(F32) for c in rows[i])
            n_sel = jnp.where(r1 == float(i), cnt, n_sel)
        sl = slice(h * P_NKEYS, (h + 1) * P_NKEYS)
        n_ref[sl, :] = n_sel
        c_ref[sl, :] = jnp.exp(s1 - v1[0]) / z
        r2_ref[sl, :] = r2.astype(BF16)
        e2_ref[sl, :] = jnp.exp(s2 - v2[0]).astype(BF16)


def _peer_scores(h, g, wq_t, k1, k2):
    T = h.shape[0]
    tm = _pick_tile(T, 256, 128)
    HK = P_HEADS * P_NKEYS
    tok = lambda i: (0, i)
    f32_side = jax.ShapeDtypeStruct((HK, T), F32)
    bf16_side = jax.ShapeDtypeStruct((HK, T), BF16)
    return pl.pallas_call(
        _peer_score_kernel,
        grid=(T // tm,),
        in_specs=[pl.BlockSpec((tm, D_MODEL), lambda i: (i, 0)),
                  pl.BlockSpec((1, D_MODEL), lambda i: (0, 0)),
                  pl.BlockSpec((P_HEADS * P_DQ, D_MODEL), lambda i: (0, 0)),
                  pl.BlockSpec((P_HEADS, P_NKEYS, P_DQ // 2), lambda i: (0, 0, 0)),
                  pl.BlockSpec((P_HEADS, P_NKEYS, P_DQ // 2), lambda i: (0, 0, 0))],
        out_specs=[pl.BlockSpec((D_MODEL, tm), tok)] + [pl.BlockSpec((HK, tm), tok)] * 4,
        out_shape=[jax.ShapeDtypeStruct((D_MODEL, T), BF16), f32_side, f32_side, bf16_side, bf16_side],
        compiler_params=_params(("parallel",)),
        name="peer_scores",
    )(h, g, wq_t, k1, k2)


PEER_TE = 1024
PEER_NJ = P_NEXP // PEER_TE
PEER_DROWS = 512


def _peer_expert_kernel(xt_ref, n_ref, c_ref, r2_ref, e2_ref, u_ref, vt_ref, o_ref, w0_ref, w1_ref, a_ref):
    j = pl.program_id(1)
    nj = pl.num_programs(1) - 1
    na = PEER_TE // P_NKEYS
    w_ref = (w0_ref, w1_ref)

    def gate_block(slot, al):
        a = j * na + al
        gate = None
        for h in range(P_HEADS):
            n_sel = n_ref[pl.ds(h * P_NKEYS + a, 1), :].astype(BF16)
            coef = c_ref[pl.ds(h * P_NKEYS + a, 1), :].astype(BF16)
            sl = slice(h * P_NKEYS, (h + 1) * P_NKEYS)
            term = jnp.where(r2_ref[sl, :] < n_sel, e2_ref[sl, :], 0.0) * coef
            gate = term if gate is None else gate + term
        w_ref[slot][al * P_NKEYS:(al + 1) * P_NKEYS, :] = gate

    def activate_block(slot, al):
        rows = slice(al * P_NKEYS, (al + 1) * P_NKEYS)
        x = a_ref[rows, :]
        act = 0.5 * x * (1.0 + lax.erf(x * (2.0 ** -0.5)))
        w_ref[slot][rows, :] = w_ref[slot][rows, :] * act.astype(BF16)

    def accumulate(slot, k):
        rows = slice(k * PEER_DROWS, (k + 1) * PEER_DROWS)
        o_ref[rows, :] += jnp.dot(vt_ref[rows, :], w_ref[slot][...], preferred_element_type=F32)

    n_acc = D_MODEL // PEER_DROWS

    @pl.when(j == 0)
    def _():
        o_ref[...] = jnp.zeros_like(o_ref)
        w1_ref[...] = jnp.zeros_like(w1_ref)

    for parity in range(2):
        @pl.when((j < nj) & (j % 2 == parity))
        def _():
            for al in range(na):
                gate_block(parity, al)
            half = PEER_TE // 2
            for r in range(2):
                rows = slice(r * half, (r + 1) * half)
                a_ref[rows, :] = jnp.dot(u_ref[rows, :], xt_ref[...], preferred_element_type=F32)
            for r in range(2):
                for al in range(r * na // 2, (r + 1) * na // 2):
                    activate_block(parity, al)
                for k in range(r * n_acc // 2, (r + 1) * n_acc // 2):
                    accumulate(1 - parity, k)

    @pl.when(j == nj)
    def _():
        for k in range(n_acc):
            accumulate((PEER_NJ - 1) % 2, k)


def _peer_experts(xt, n_sel, coef, r2, e2, u, vt):
    T = xt.shape[1]
    tm = _pick_tile(T, 768, 256)
    te = PEER_TE
    nj = PEER_NJ
    HK = P_HEADS * P_NKEYS
    once = pl.Buffered(1) if tm > 512 else None
    tok = lambda i, j: (0, i)
    return pl.pallas_call(
        _peer_expert_kernel,
        grid=(T // tm, nj + 1),
        in_specs=[pl.BlockSpec((D_MODEL, tm), tok, pipeline_mode=once)]
                 + [pl.BlockSpec((HK, tm), tok, pipeline_mode=once)] * 4
                 + [pl.BlockSpec((te, D_MODEL), lambda i, j: (jnp.minimum(j, nj - 1), 0)),
                    pl.BlockSpec((None, D_MODEL, te), lambda i, j: (jnp.maximum(j - 1, 0), 0, 0))],
        out_specs=pl.BlockSpec((D_MODEL, tm), tok),
        out_shape=jax.ShapeDtypeStruct((D_MODEL, T), F32),
        scratch_shapes=[pltpu.VMEM((te, tm), BF16), pltpu.VMEM((te, tm), BF16), pltpu.VMEM((te, tm), F32)],
        compiler_params=_params(("parallel", "arbitrary")),
        name="peer_experts",
    )(xt, n_sel, coef, r2, e2, u, vt)


def _add_transposed_kernel(h_ref, yt_ref, o_ref):
    o_ref[...] = h_ref[...] + yt_ref[...].T


def _add_transposed(h, yt):
    T, D = h.shape
    tm = _pick_tile(T, 512, 128)
    return pl.pallas_call(
        _add_transposed_kernel,
        grid=(T // tm,),
        in_specs=[pl.BlockSpec((tm, D), lambda i: (i, 0)), pl.BlockSpec((D, tm), lambda i: (0, i))],
        out_specs=pl.BlockSpec((tm, D), lambda i: (i, 0)),
        out_shape=jax.ShapeDtypeStruct((T, D), F32),
        compiler_params=_params(("parallel",)),
        name="add_transposed",
    )(h, yt)


def _prep_layer(l, g_mix, w_in, b_mgate, g_mhead, g_q, g_k, attn_sink, w_branch_m, w_branch_a,
                b_merge, w_out, g_ffn, w_pq, sub_k1, sub_k2, u_tab, v_tab):
    n_gate0 = 2 * M_QK + 2 * M_V
    n_gm0 = n_gate0 + N_GATE + A_Q + 2 * A_KV
    w = w_in[l]
    w_main = jnp.concatenate([w[:, n_gm0:], w[:, :n_gate0], w[:, n_gate0 + N_GATE:n_gm0]], axis=1).astype(BF16)
    w_gate = jnp.pad(w[:, n_gate0:n_gate0 + N_GATE], ((0, 0), (0, LANES - N_GATE))).astype(BF16)
    bias = jnp.pad(b_mgate[l], (0, LANES - N_GATE)).reshape(1, LANES)
    return dict(
        g_mix=g_mix[l].reshape(1, D_MODEL), w_main=_column_tiles(w_main, 1024),
        w_gate=_column_tiles(w_gate, LANES), bias=bias,
        g_mhead=g_mhead[l].reshape(1, M_V), g_q=g_q[l].reshape(1, A_DH), g_k=g_k[l].reshape(1, A_DH),
        sink=attn_sink[l], wm=w_branch_m[l].astype(BF16), wa=w_branch_a[l].astype(BF16),
        b_merge=b_merge[l].reshape(1, 2 * D_MODEL), wo=w_out[l].astype(BF16),
        g_ffn=g_ffn[l].reshape(1, D_MODEL), wq_t=w_pq[l].T.astype(BF16),
        k1=sub_k1[l].astype(BF16), k2=sub_k2[l].astype(BF16),
        u=u_tab[l].astype(BF16),
        vt=v_tab[l].astype(BF16).reshape(PEER_NJ, PEER_TE, D_MODEL).transpose(0, 2, 1))


def _layer(h, p, cos, sin, B, nc):
    proj = _norm_proj(h, p["g_mix"], p["w_main"], BF16)
    gates = _norm_proj(h, p["g_mix"], p["w_gate"], F32)
    hf, hb = _mlstm(proj, gates, p["bias"], B, nc)
    q, k, v = _qk_prep(proj, cos, sin, p["g_q"], p["g_k"])
    oa = _attention(q, k, v, p["sink"], B, nc)
    z = _merge(hf, hb, proj, p["g_mhead"], oa, p["b_merge"], p["wm"], p["wa"])
    h = _out_proj(z, p["wo"], h)
    xt, n_sel, coef, r2, e2 = _peer_scores(h, p["g_ffn"], p["wq_t"], p["k1"], p["k2"])
    return _add_transposed(h, _peer_experts(xt, n_sel, coef, r2, e2, p["u"], p["vt"]))


def _trunk(x, meta_tokens, layers):
    B, L, D = x.shape
    Lp = BLOCK + L
    nc = Lp // BLOCK
    meta = jnp.broadcast_to(meta_tokens.astype(x.dtype)[None], (B, N_META, D))
    h = jnp.concatenate([jnp.zeros((B, META_START, D), x.dtype), meta, x], axis=1).reshape(B * Lp, D)
    pos = (jnp.arange(Lp) - META_START).astype(F32)
    half = A_DH // 2
    inv = ROPE_THETA ** (-jnp.arange(half, dtype=F32) / half)
    ang = pos[:, None] * inv[None, :]
    cos = jnp.tile(jnp.concatenate([jnp.cos(ang), jnp.cos(ang)], axis=1), (B, 1))
    sin = jnp.tile(jnp.concatenate([-jnp.sin(ang), jnp.sin(ang)], axis=1), (B, 1))
    for p in layers:
        h = _layer(h, p, cos, sin, B, nc)
    return h.reshape(B, Lp, D)[:, BLOCK:]


def kernel(x_prompt, x_sample, meta_tokens, g_mix, w_in, b_mgate, g_mhead, g_q, g_k, attn_sink,
           w_branch_m, w_branch_a, b_merge, w_out, g_ffn, w_pq, sub_k1, sub_k2, u_tab, v_tab):
    weights = (g_mix, w_in, b_mgate, g_mhead, g_q, g_k, attn_sink, w_branch_m, w_branch_a,
               b_merge, w_out, g_ffn, w_pq, sub_k1, sub_k2, u_tab, v_tab)
    layers = [_prep_layer(l, *weights) for l in range(w_in.shape[0])]
    return (_trunk(x_prompt, meta_tokens, layers), _trunk(x_sample, meta_tokens, layers))
```

```python
import functools

import jax
import jax.numpy as jnp
from jax import lax
from jax.experimental import pallas as pl
from jax.experimental.pallas import tpu as pltpu

F32 = jnp.float32
BF16 = jnp.bfloat16

D_MODEL = 2048
DEPTH = 4
N_META = 16
BLOCK = 128
META_START = BLOCK - N_META
EPS = 1e-6
M_HEADS = 8
M_DQK = 128
M_DV = 256
A_HEADS = 16
A_KV_HEADS = 4
A_GROUP = A_HEADS // A_KV_HEADS
A_DH = 128
WINDOW = 128
ROPE_THETA = 10000.0
P_HEADS = 8
P_NKEYS = 128
P_NEXP = P_NKEYS * P_NKEYS
P_DQ = 256
P_TOPK = 16

M_QK = M_HEADS * M_DQK
M_V = M_HEADS * M_DV
A_Q = A_HEADS * A_DH
A_KV = A_KV_HEADS * A_DH
OFF_GM = 0
OFF_MQ = OFF_GM + 2 * D_MODEL
OFF_MK = OFF_MQ + M_QK
OFF_MV = OFF_MK + M_QK
OFF_MO = OFF_MV + M_V
OFF_AQ = OFF_MO + M_V
OFF_AK = OFF_AQ + A_Q
OFF_AV = OFF_AK + A_KV
N_MAIN = OFF_AV + A_KV
N_GATE = 4 * M_HEADS
LANES = 128
STATE_W = M_DV + LANES
VMEM_LIMIT = 56 * 1024 * 1024


def _pick_tile(n, cap, mult):
    best = None
    for t in range(mult, min(n, cap) + 1, mult):
        if n % t == 0:
            best = t
    assert best is not None, (n, cap, mult)
    return best


def _params(sem):
    return pltpu.CompilerParams(dimension_semantics=sem, vmem_limit_bytes=VMEM_LIMIT)


def _norm_proj_kernel(h_ref, g_ref, w_ref, o_ref, xn_ref):
    @pl.when(pl.program_id(1) == 0)
    def _():
        x = h_ref[...]
        ms = jnp.mean(x * x, axis=-1, keepdims=True)
        xn_ref[...] = (x * lax.rsqrt(ms + EPS) * g_ref[...]).astype(BF16)

    o_ref[...] = jnp.dot(xn_ref[...], w_ref[...], preferred_element_type=F32).astype(o_ref.dtype)


def _column_tiles(w, tn):
    K, N = w.shape
    return w.reshape(K, N // tn, tn).transpose(1, 0, 2)


def _norm_proj(h, g, w_tiles, out_dtype):
    T, D = h.shape
    nj, _, tn = w_tiles.shape
    tm = _pick_tile(T, 1024, 256)
    return pl.pallas_call(
        _norm_proj_kernel,
        grid=(T // tm, nj),
        in_specs=[pl.BlockSpec((tm, D), lambda i, j: (i, 0)),
                  pl.BlockSpec((1, D), lambda i, j: (0, 0)),
                  pl.BlockSpec((None, D, tn), lambda i, j: (j, 0, 0))],
        out_specs=pl.BlockSpec((tm, tn), lambda i, j: (i, j)),
        out_shape=jax.ShapeDtypeStruct((T, nj * tn), out_dtype),
        scratch_shapes=[pltpu.VMEM((tm, D), BF16)],
        compiler_params=_params(("parallel", "arbitrary")),
        name="norm_proj",
    )(h, g, w_tiles)


def _mlstm_direction(q_ref, k_ref, v_ref, g_ref, bias_ref, o_ref, s_ref, m_ref, tok0, fwd):
    T = BLOCK
    g = g_ref[...] + bias_ref[...]
    rows = lax.broadcasted_iota(jnp.int32, (T, LANES), 0)
    valid = (tok0 + rows) >= META_START
    logsig = jnp.minimum(g, 0.0) - jnp.log1p(jnp.exp(-jnp.abs(g)))
    lf = jnp.where(valid, logsig, 0.0)
    li = jnp.where(valid, g, -jnp.inf)
    r = lax.broadcasted_iota(jnp.int32, (T, T), 0)
    c = lax.broadcasted_iota(jnp.int32, (T, T), 1)
    tri = (c <= r) if fwd else (c >= r)
    bcum = jnp.dot(tri.astype(F32), lf, preferred_element_type=F32,
                   precision=lax.Precision.HIGHEST)
    bcum_t = bcum.T
    li_t = li.T
    i_off = 0 if fwd else 2 * M_HEADS
    f_off = i_off + M_HEADS
    last = T - 1 if fwd else 0
    ones = jnp.ones((T, LANES), BF16)
    for h in range(M_HEADS):
        b_col = bcum[:, f_off + h:f_off + h + 1]
        b_row = bcum_t[f_off + h:f_off + h + 1, :]
        li_col = li[:, i_off + h:i_off + h + 1]
        li_row = li_t[i_off + h:i_off + h + 1, :]
        b_last = bcum[last:last + 1, f_off + h:f_off + h + 1]
        m_old = m_ref[h:h + 1, 0:1]
        qb = q_ref[:, h * M_DQK:(h + 1) * M_DQK].astype(BF16)
        kb = (k_ref[:, h * M_DQK:(h + 1) * M_DQK] * (M_DQK ** -0.5)).astype(BF16)
        v_ext = jnp.concatenate([v_ref[:, h * M_DV:(h + 1) * M_DV], ones], axis=1)
        state = s_ref[h]

        dm = jnp.where(tri, b_col - b_row + li_row, -jnp.inf)
        inter = b_col + m_old
        m_t = jnp.maximum(inter, jnp.max(dm, axis=1, keepdims=True))
        s_qk = lax.dot_general(qb, kb, (((1,), (1,)), ((), ())), preferred_element_type=F32)
        p = (jnp.exp(dm - m_t) * s_qk).astype(BF16)
        w_inter = jnp.exp(inter - m_t)
        lhs = jnp.concatenate([p, (w_inter * qb).astype(BF16)], axis=1)
        rhs = jnp.concatenate([v_ext, state.astype(BF16)], axis=0)
        nd = jnp.dot(lhs, rhs, preferred_element_type=F32)
        den = jnp.maximum(jnp.abs(nd[:, M_DV:]), jnp.exp(-m_t))
        inv = 1.0 / den
        o_ref[:, h * M_DV:(h + 1) * M_DV] = (
            nd[:, :M_DV] * jnp.concatenate([inv, inv], axis=1)).astype(o_ref.dtype)

        gk = b_last - b_col + li_col
        m_new = jnp.maximum(b_last + m_old, jnp.max(gk, axis=0, keepdims=True))
        wk = jnp.exp(gk - m_new)
        decay = jnp.exp(b_last + m_old - m_new)
        upd = lax.dot_general(kb, (wk * v_ext).astype(BF16), (((0,), (0,)), ((), ())),
                              preferred_element_type=F32)
        s_ref[h] = decay * state + upd
        m_ref[h:h + 1, :] = jnp.broadcast_to(m_new, (1, LANES))


def _mlstm_kernel(qf, kf, vf, gf, qb, kb, vb, gb, bias, of, ob, sf, sb, mf, mb):
    c = pl.program_id(1)
    nc = pl.num_programs(1)

    @pl.when(c == 0)
    def _():
        sf[...] = jnp.zeros_like(sf)
        sb[...] = jnp.zeros_like(sb)
        mf[...] = jnp.zeros_like(mf)
        mb[...] = jnp.zeros_like(mb)

    _mlstm_direction(qf, kf, vf, gf, bias, of, sf, mf, c * BLOCK, True)
    _mlstm_direction(qb, kb, vb, gb, bias, ob, sb, mb, (nc - 1 - c) * BLOCK, False)


def _mlstm(proj, gates, bias, B, nc):
    T = proj.shape[0]

    def fwd_map(col):
        return lambda b, c: (b * nc + c, col)

    def bwd_map(col):
        return lambda b, c: (b * nc + nc - 1 - c, col)

    def specs(mk):
        return [pl.BlockSpec((BLOCK, M_QK), mk(OFF_MQ // M_QK)),
                pl.BlockSpec((BLOCK, M_QK), mk(OFF_MK // M_QK)),
                pl.BlockSpec((BLOCK, M_V), mk(OFF_MV // M_V))]

    gate_f = pl.BlockSpec((BLOCK, LANES), fwd_map(0))
    gate_b = pl.BlockSpec((BLOCK, LANES), bwd_map(0))
    out = jax.ShapeDtypeStruct((T, M_V), BF16)
    return pl.pallas_call(
        _mlstm_kernel,
        grid=(B, nc),
        in_specs=(specs(fwd_map) + [gate_f] + specs(bwd_map) + [gate_b]
                  + [pl.BlockSpec((1, LANES), lambda b, c: (0, 0))]),
        out_specs=[pl.BlockSpec((BLOCK, M_V), fwd_map(0)), pl.BlockSpec((BLOCK, M_V), bwd_map(0))],
        out_shape=[out, out],
        scratch_shapes=[pltpu.VMEM((M_HEADS, M_DQK, STATE_W), F32),
                        pltpu.VMEM((M_HEADS, M_DQK, STATE_W), F32),
                        pltpu.VMEM((M_HEADS, LANES), F32),
                        pltpu.VMEM((M_HEADS, LANES), F32)],
        compiler_params=_params(("parallel", "arbitrary")),
        name="mlstm",
    )(proj, proj, proj, gates, proj, proj, proj, gates, bias)


def _qk_prep_kernel(q_ref, k_ref, v_ref, cos_ref, sin_ref, gq_ref, gk_ref, qo_ref, ko_ref, vo_ref):
    cos = cos_ref[...]
    sin = sin_ref[...]

    def norm_rope(x, g):
        xn = x * lax.rsqrt(jnp.mean(x * x, axis=-1, keepdims=True) + EPS) * g
        return xn * cos + pltpu.roll(xn, A_DH // 2, axis=1) * sin

    for h in range(A_HEADS):
        sl = slice(h * A_DH, (h + 1) * A_DH)
        qo_ref[:, sl] = (norm_rope(q_ref[:, sl].astype(F32), gq_ref[...]) * (A_DH ** -0.5)).astype(BF16)
    for h in range(A_KV_HEADS):
        sl = slice(h * A_DH, (h + 1) * A_DH)
        ko_ref[:, sl] = norm_rope(k_ref[:, sl].astype(F32), gk_ref[...]).astype(BF16)
    vo_ref[...] = v_ref[...].astype(BF16)


def _qk_prep(proj, cos, sin, g_q, g_k):
    T = proj.shape[0]
    tm = _pick_tile(T, 512, 128)
    row = lambda col: (lambda i: (i, col))
    const = lambda i: (0, 0)
    return pl.pallas_call(
        _qk_prep_kernel,
        grid=(T // tm,),
        in_specs=[pl.BlockSpec((tm, A_Q), row(OFF_AQ // A_Q)),
                  pl.BlockSpec((tm, A_KV), row(OFF_AK // A_KV)),
                  pl.BlockSpec((tm, A_KV), row(OFF_AV // A_KV)),
                  pl.BlockSpec((tm, A_DH), row(0)),
                  pl.BlockSpec((tm, A_DH), row(0)),
                  pl.BlockSpec((1, A_DH), const),
                  pl.BlockSpec((1, A_DH), const)],
        out_specs=[pl.BlockSpec((tm, A_Q), row(0)),
                   pl.BlockSpec((tm, A_KV), row(0)),
                   pl.BlockSpec((tm, A_KV), row(0))],
        out_shape=[jax.ShapeDtypeStruct((T, A_Q), BF16),
                   jax.ShapeDtypeStruct((T, A_KV), BF16),
                   jax.ShapeDtypeStruct((T, A_KV), BF16)],
        compiler_params=_params(("parallel",)),
        name="qk_prep",
    )(proj, proj, proj, cos, sin, g_q, g_k)


def _attn_kernel(sink_ref, q_ref, kp, kc, kn, km, vp, vc, vn, vm, o_ref):
    n = pl.program_id(1)
    nb = pl.num_programs(1)
    R = A_GROUP * BLOCK
    W = 4 * BLOCK
    row = lax.broadcasted_iota(jnp.int32, (R, W), 0)
    col = lax.broadcasted_iota(jnp.int32, (R, W), 1)
    t = row % BLOCK
    seg = col // BLOCK
    off = col % BLOCK
    dist = (1 - seg) * BLOCK + (t - off)
    blk = n - 1 + seg
    win_ok = (seg < 3) & (jnp.abs(dist) <= WINDOW) & (blk >= 1) & (blk <= nb - 1)
    meta_ok = (seg == 3) & (off >= META_START)
    mask = win_ok | meta_ok
    head_row = lax.broadcasted_iota(jnp.int32, (R, 1), 0) // BLOCK
    for j in range(A_KV_HEADS):
        sl = slice(j * A_DH, (j + 1) * A_DH)
        q4 = jnp.concatenate([q_ref[:, (j * A_GROUP + g) * A_DH:(j * A_GROUP + g + 1) * A_DH]
                              for g in range(A_GROUP)], axis=0)
        kcat = jnp.concatenate([kp[:, sl], kc[:, sl], kn[:, sl], km[:, sl]], axis=0)
        vcat = jnp.concatenate([vp[:, sl], vc[:, sl], vn[:, sl], vm[:, sl]], axis=0)
        s = lax.dot_general(q4, kcat, (((1,), (1,)), ((), ())), preferred_element_type=F32)
        s = jnp.where(mask, s, -jnp.inf)
        sink = jnp.zeros((R, 1), F32)
        for g in range(A_GROUP):
            sink = jnp.where(head_row == g, sink_ref[j * A_GROUP + g], sink)
        m = jnp.maximum(jnp.max(s, axis=1, keepdims=True), sink)
        p = jnp.exp(s - m)
        denom = jnp.sum(p, axis=1, keepdims=True) + jnp.exp(sink - m)
        o = jnp.dot(p.astype(BF16), vcat, preferred_element_type=F32) / denom
        for g in range(A_GROUP):
            hh = j * A_GROUP + g
            o_ref[:, hh * A_DH:(hh + 1) * A_DH] = o[g * BLOCK:(g + 1) * BLOCK].astype(BF16)


def _attention(q, k, v, sink, B, nb):
    T = q.shape[0]
    cur = lambda b, n, s: (b * nb + n, 0)
    prev = lambda b, n, s: (b * nb + jnp.maximum(n - 1, 0), 0)
    nxt = lambda b, n, s: (b * nb + jnp.minimum(n + 1, nb - 1), 0)
    meta = lambda b, n, s: (b * nb, 0)
    kv = lambda m: pl.BlockSpec((BLOCK, A_KV), m)
    grid_spec = pltpu.PrefetchScalarGridSpec(
        num_scalar_prefetch=1,
        grid=(B, nb),
        in_specs=[pl.BlockSpec((BLOCK, A_Q), cur),
                  kv(prev), kv(cur), kv(nxt), kv(meta), kv(prev), kv(cur), kv(nxt), kv(meta)],
        out_specs=pl.BlockSpec((BLOCK, A_Q), cur))
    return pl.pallas_call(
        _attn_kernel,
        grid_spec=grid_spec,
        out_shape=jax.ShapeDtypeStruct((T, A_Q), BF16),
        compiler_params=_params(("parallel", "arbitrary")),
        name="band_attn",
    )(sink, q, k, k, k, k, v, v, v, v)


MERGE_TN = 512


def _merge_kernel(hf_ref, hb_ref, mo_ref, gh_ref, oa_ref, gm_ref, b_ref, wm_ref, wa_ref, z_ref, hm_ref):
    for h in range(M_HEADS):
        sl = slice(h * M_DV, (h + 1) * M_DV)
        x = hf_ref[:, sl].astype(F32) + hb_ref[:, sl].astype(F32)
        xn = x * lax.rsqrt(jnp.mean(x * x, axis=-1, keepdims=True) + EPS) * gh_ref[:, sl]
        hm_ref[:, sl] = (xn * jax.nn.sigmoid(mo_ref[:, sl].astype(F32))).astype(BF16)
    for j in range(D_MODEL // MERGE_TN):
        cm = slice(j * MERGE_TN, (j + 1) * MERGE_TN)
        ca = slice(D_MODEL + j * MERGE_TN, D_MODEL + (j + 1) * MERGE_TN)
        y_m = jnp.dot(hm_ref[...], wm_ref[:, cm], preferred_element_type=F32)
        y_a = jnp.dot(oa_ref[...], wa_ref[:, cm], preferred_element_type=F32)
        gate_m = jax.nn.sigmoid(gm_ref[:, cm].astype(F32) + b_ref[:, cm])
        gate_a = jax.nn.sigmoid(gm_ref[:, ca].astype(F32) + b_ref[:, ca])
        z_ref[:, cm] = (gate_m * y_m + gate_a * y_a).astype(BF16)


def _merge(hf, hb, proj, g_mhead, oa, b_merge, wm, wa):
    T = hf.shape[0]
    tm = _pick_tile(T, 512, 128)
    row = lambda col: (lambda i: (i, col))
    const = lambda i: (0, 0)
    once = pl.Buffered(1)
    return pl.pallas_call(
        _merge_kernel,
        grid=(T // tm,),
        in_specs=[pl.BlockSpec((tm, M_V), row(0)),
                  pl.BlockSpec((tm, M_V), row(0)),
                  pl.BlockSpec((tm, M_V), row(OFF_MO // M_V)),
                  pl.BlockSpec((1, M_V), const),
                  pl.BlockSpec((tm, A_Q), row(0)),
                  pl.BlockSpec((tm, 2 * D_MODEL), row(OFF_GM // (2 * D_MODEL))),
                  pl.BlockSpec((1, 2 * D_MODEL), const),
                  pl.BlockSpec((M_V, D_MODEL), const, pipeline_mode=once),
                  pl.BlockSpec((A_Q, D_MODEL), const, pipeline_mode=once)],
        out_specs=pl.BlockSpec((tm, D_MODEL), row(0)),
        out_shape=jax.ShapeDtypeStruct((T, D_MODEL), BF16),
        scratch_shapes=[pltpu.VMEM((tm, M_V), BF16)],
        compiler_params=_params(("parallel",)),
        name="merge",
    )(hf, hb, proj, g_mhead, oa, proj, b_merge, wm, wa)


def _out_proj_kernel(z_ref, w_ref, h_ref, o_ref):
    for j in range(D_MODEL // MERGE_TN):
        cs = slice(j * MERGE_TN, (j + 1) * MERGE_TN)
        o_ref[:, cs] = h_ref[:, cs] + jnp.dot(z_ref[...], w_ref[:, cs], preferred_element_type=F32)


def _out_proj(z, w, h):
    T = z.shape[0]
    tm = _pick_tile(T, 1024, 256)
    row = lambda i: (i, 0)
    return pl.pallas_call(
        _out_proj_kernel,
        grid=(T // tm,),
        in_specs=[pl.BlockSpec((tm, D_MODEL), row),
                  pl.BlockSpec((D_MODEL, D_MODEL), lambda i: (0, 0), pipeline_mode=pl.Buffered(1)),
                  pl.BlockSpec((tm, D_MODEL), row)],
        out_specs=pl.BlockSpec((tm, D_MODEL), row),
        out_shape=jax.ShapeDtypeStruct((T, D_MODEL), F32),
        compiler_params=_params(("parallel",)),
        name="out_proj",
    )(z, w, h)


def _top_ranked(s, count):
    vals = []
    rank = jnp.full(s.shape, float(count), F32)
    for i in range(count):
        m = jnp.max(s, axis=0, keepdims=True)
        vals.append(m)
        hit = s >= m
        rank = jnp.where(hit, float(i), rank)
        s = jnp.where(hit, -jnp.inf, s)
    return vals, rank


def _top_values(s, count):
    m = jnp.max(s, axis=0, keepdims=True)
    vals = [m]
    for _ in range(count - 1):
        m = jnp.max(jnp.where(s < m, s, -jnp.inf), axis=0, keepdims=True)
        vals.append(m)
    return vals


def _peer_score_kernel(h_ref, g_ref, wq_ref, k1_ref, k2_ref, xt_ref, n_ref, c_ref, r2_ref, e2_ref):
    x = h_ref[...]
    x = x * lax.rsqrt(jnp.mean(x * x, axis=-1, keepdims=True) + EPS) * g_ref[...]
    xt = x.T.astype(BF16)
    xt_ref[...] = xt
    qt = jnp.dot(wq_ref[...], xt, preferred_element_type=F32)
    half = P_DQ // 2
    for h in range(P_HEADS):
        q1 = qt[h * P_DQ:h * P_DQ + half].astype(BF16)
        q2 = qt[h * P_DQ + half:(h + 1) * P_DQ].astype(BF16)
        s1 = jnp.dot(k1_ref[h], q1, preferred_element_type=F32)
        s2 = jnp.dot(k2_ref[h], q2, preferred_element_type=F32)
        v1 = _top_values(s1, P_TOPK)
        v2, r2 = _top_ranked(s2, P_TOPK)
        rows = [[v1[i] + v2[j] for j in range(P_TOPK // (i + 1))] for i in range(P_TOPK)]
        cand = jnp.concatenate([c for row in rows for c in row], axis=0)
        s = cand
        for _ in range(P_TOPK):
            c16 = jnp.max(s, axis=0, keepdims=True)
            s = jnp.where(s >= c16, -jnp.inf, s)
        cmax = v1[0] + v2[0]
        z = jnp.sum(jnp.where(cand >= c16, jnp.exp(cand - cmax), 0.0), axis=0, keepdims=True)
        n_sel = jnp.zeros_like(s1)
        for j in range(P_TOPK):
            n_sel = jnp.where(s1 + v2[j] >= c16, float(j + 1), n_sel)
        sl = slice(h * P_NKEYS, (h + 1) * P_NKEYS)
        n_ref[sl, :] = n_sel
        c_ref[sl, :] = jnp.exp(s1 - v1[0]) / z
        r2_ref[sl, :] = r2.astype(BF16)
        e2_ref[sl, :] = jnp.exp(s2 - v2[0]).astype(BF16)


def _peer_scores(h, g, wq_t, k1, k2):
    T = h.shape[0]
    tm = _pick_tile(T, 256, 128)
    HK = P_HEADS * P_NKEYS
    tok = lambda i: (0, i)
    f32_side = jax.ShapeDtypeStruct((HK, T), F32)
    bf16_side = jax.ShapeDtypeStruct((HK, T), BF16)
    return pl.pallas_call(
        _peer_score_kernel,
        grid=(T // tm,),
        in_specs=[pl.BlockSpec((tm, D_MODEL), lambda i: (i, 0)),
                  pl.BlockSpec((1, D_MODEL), lambda i: (0, 0)),
                  pl.BlockSpec((P_HEADS * P_DQ, D_MODEL), lambda i: (0, 0)),
                  pl.BlockSpec((P_HEADS, P_NKEYS, P_DQ // 2), lambda i: (0, 0, 0)),
                  pl.BlockSpec((P_HEADS, P_NKEYS, P_DQ // 2), lambda i: (0, 0, 0))],
        out_specs=[pl.BlockSpec((D_MODEL, tm), tok)] + [pl.BlockSpec((HK, tm), tok)] * 4,
        out_shape=[jax.ShapeDtypeStruct((D_MODEL, T), BF16), f32_side, f32_side, bf16_side, bf16_side],
        compiler_params=_params(("parallel",)),
        name="peer_scores",
    )(h, g, wq_t, k1, k2)


PEER_TE = 1024
PEER_NJ = P_NEXP // PEER_TE
PEER_DROWS = 512


def _peer_expert_kernel(xt_ref, n_ref, c_ref, r2_ref, e2_ref, u_ref, vt_ref, o_ref, w0_ref, w1_ref, a_ref):
    j = pl.program_id(1)
    nj = pl.num_programs(1) - 1
    na = PEER_TE // P_NKEYS
    w_ref = (w0_ref, w1_ref)

    def gate_block(slot, al):
        a = j * na + al
        gate = None
        for h in range(P_HEADS):
            n_sel = n_ref[pl.ds(h * P_NKEYS + a, 1), :].astype(BF16)
            coef = c_ref[pl.ds(h * P_NKEYS + a, 1), :].astype(BF16)
            sl = slice(h * P_NKEYS, (h + 1) * P_NKEYS)
            term = jnp.where(r2_ref[sl, :] < n_sel, e2_ref[sl, :], 0.0) * coef
            gate = term if gate is None else gate + term
        w_ref[slot][al * P_NKEYS:(al + 1) * P_NKEYS, :] = gate

    def activate_block(slot, al):
        rows = slice(al * P_NKEYS, (al + 1) * P_NKEYS)
        x = a_ref[rows, :]
        act = 0.5 * x * (1.0 + lax.erf(x * (2.0 ** -0.5)))
        w_ref[slot][rows, :] = w_ref[slot][rows, :] * act

    def accumulate(slot, k):
        rows = slice(k * PEER_DROWS, (k + 1) * PEER_DROWS)
        o_ref[rows, :] += jnp.dot(vt_ref[rows, :], w_ref[slot][...], preferred_element_type=F32)

    n_acc = D_MODEL // PEER_DROWS

    @pl.when(j == 0)
    def _():
        o_ref[...] = jnp.zeros_like(o_ref)
        w1_ref[...] = jnp.zeros_like(w1_ref)

    for parity in range(2):
        @pl.when((j < nj) & (j % 2 == parity))
        def _():
            for al in range(na):
                gate_block(parity, al)
            half = PEER_TE // 2
            for r in range(2):
                rows = slice(r * half, (r + 1) * half)
                a_ref[rows, :] = jnp.dot(u_ref[rows, :], xt_ref[...],
                                         preferred_element_type=F32).astype(BF16)
            for r in range(2):
                for al in range(r * na // 2, (r + 1) * na // 2):
                    activate_block(parity, al)
                for k in range(r * n_acc // 2, (r + 1) * n_acc // 2):
                    accumulate(1 - parity, k)

    @pl.when(j == nj)
    def _():
        for k in range(n_acc):
            accumulate((PEER_NJ - 1) % 2, k)


def _peer_experts(xt, n_sel, coef, r2, e2, u, vt):
    T = xt.shape[1]
    tm = _pick_tile(T, 768, 256)
    te = PEER_TE
    nj = PEER_NJ
    HK = P_HEADS * P_NKEYS
    once = pl.Buffered(1) if tm > 512 else None
    tok = lambda i, j: (0, i)
    return pl.pallas_call(
        _peer_expert_kernel,
        grid=(T // tm, nj + 1),
        in_specs=[pl.BlockSpec((D_MODEL, tm), tok, pipeline_mode=once)]
                 + [pl.BlockSpec((HK, tm), tok, pipeline_mode=once)] * 4
                 + [pl.BlockSpec((te, D_MODEL), lambda i, j: (jnp.minimum(j, nj - 1), 0)),
                    pl.BlockSpec((None, D_MODEL, te), lambda i, j: (jnp.maximum(j - 1, 0), 0, 0))],
        out_specs=pl.BlockSpec((D_MODEL, tm), tok),
        out_shape=jax.ShapeDtypeStruct((D_MODEL, T), F32),
        scratch_shapes=[pltpu.VMEM((te, tm), BF16), pltpu.VMEM((te, tm), BF16), pltpu.VMEM((te, tm), BF16)],
        compiler_params=_params(("parallel", "arbitrary")),
        name="peer_experts",
    )(xt, n_sel, coef, r2, e2, u, vt)


def _add_transposed_kernel(h_ref, yt_ref, o_ref):
    o_ref[...] = h_ref[...] + yt_ref[...].T


def _add_transposed(h, yt):
    T, D = h.shape
    tm = _pick_tile(T, 512, 128)
    return pl.pallas_call(
        _add_transposed_kernel,
        grid=(T // tm,),
        in_specs=[pl.BlockSpec((tm, D), lambda i: (i, 0)), pl.BlockSpec((D, tm), lambda i: (0, i))],
        out_specs=pl.BlockSpec((tm, D), lambda i: (i, 0)),
        out_shape=jax.ShapeDtypeStruct((T, D), F32),
        compiler_params=_params(("parallel",)),
        name="add_transposed",
    )(h, yt)


def _prep_layer(l, g_mix, w_in, b_mgate, g_mhead, g_q, g_k, attn_sink, w_branch_m, w_branch_a,
                b_merge, w_out, g_ffn, w_pq, sub_k1, sub_k2, u_tab, v_tab):
    n_gate0 = 2 * M_QK + 2 * M_V
    n_gm0 = n_gate0 + N_GATE + A_Q + 2 * A_KV
    w = w_in[l]
    w_main = jnp.concatenate([w[:, n_gm0:], w[:, :n_gate0], w[:, n_gate0 + N_GATE:n_gm0]], axis=1).astype(BF16)
    w_gate = jnp.pad(w[:, n_gate0:n_gate0 + N_GATE], ((0, 0), (0, LANES - N_GATE))).astype(BF16)
    bias = jnp.pad(b_mgate[l], (0, LANES - N_GATE)).reshape(1, LANES)
    return dict(
        g_mix=g_mix[l].reshape(1, D_MODEL), w_main=_column_tiles(w_main, 1024),
        w_gate=_column_tiles(w_gate, LANES), bias=bias,
        g_mhead=g_mhead[l].reshape(1, M_V), g_q=g_q[l].reshape(1, A_DH), g_k=g_k[l].reshape(1, A_DH),
        sink=attn_sink[l], wm=w_branch_m[l].astype(BF16), wa=w_branch_a[l].astype(BF16),
        b_merge=b_merge[l].reshape(1, 2 * D_MODEL), wo=w_out[l].astype(BF16),
        g_ffn=g_ffn[l].reshape(1, D_MODEL), wq_t=w_pq[l].T.astype(BF16),
        k1=sub_k1[l].astype(BF16), k2=sub_k2[l].astype(BF16),
        u=u_tab[l].astype(BF16),
        vt=v_tab[l].astype(BF16).reshape(PEER_NJ, PEER_TE, D_MODEL).transpose(0, 2, 1))


def _layer(h, p, cos, sin, B, nc):
    proj = _norm_proj(h, p["g_mix"], p["w_main"], BF16)
    gates = _norm_proj(h, p["g_mix"], p["w_gate"], F32)
    hf, hb = _mlstm(proj, gates, p["bias"], B, nc)
    q, k, v = _qk_prep(proj, cos, sin, p["g_q"], p["g_k"])
    oa = _attention(q, k, v, p["sink"], B, nc)
    z = _merge(hf, hb, proj, p["g_mhead"], oa, p["b_merge"], p["wm"], p["wa"])
    h = _out_proj(z, p["wo"], h)
    xt, n_sel, coef, r2, e2 = _peer_scores(h, p["g_ffn"], p["wq_t"], p["k1"], p["k2"])
    return _add_transposed(h, _peer_experts(xt, n_sel, coef, r2, e2, p["u"], p["vt"]))


def _trunk(x, meta_tokens, layers):
    B, L, D = x.shape
    Lp = BLOCK + L
    nc = Lp // BLOCK
    meta = jnp.broadcast_to(meta_tokens.astype(x.dtype)[None], (B, N_META, D))
    h = jnp.concatenate([jnp.zeros((B, META_START, D), x.dtype), meta, x], axis=1).reshape(B * Lp, D)
    pos = (jnp.arange(Lp) - META_START).astype(F32)
    half = A_DH // 2
    inv = ROPE_THETA ** (-jnp.arange(half, dtype=F32) / half)
    ang = pos[:, None] * inv[None, :]
    cos = jnp.tile(jnp.concatenate([jnp.cos(ang), jnp.cos(ang)], axis=1), (B, 1))
    sin = jnp.tile(jnp.concatenate([-jnp.sin(ang), jnp.sin(ang)], axis=1), (B, 1))
    for p in layers:
        h = _layer(h, p, cos, sin, B, nc)
    return h.reshape(B, Lp, D)[:, BLOCK:]


def kernel(x_prompt, x_sample, meta_tokens, g_mix, w_in, b_mgate, g_mhead, g_q, g_k, attn_sink,
           w_branch_m, w_branch_a, b_merge, w_out, g_ffn, w_pq, sub_k1, sub_k2, u_tab, v_tab):
    weights = (g_mix, w_in, b_mgate, g_mhead, g_q, g_k, attn_sink, w_branch_m, w_branch_a,
               b_merge, w_out, g_ffn, w_pq, sub_k1, sub_k2, u_tab, v_tab)
    layers = [_prep_layer(l, *weights) for l in range(w_in.shape[0])]
    return (_trunk(x_prompt, meta_tokens, layers), _trunk(x_sample, meta_tokens, layers))
```

```python
import functools

import jax
import jax.numpy as jnp
from jax import lax
from jax.experimental import pallas as pl
from jax.experimental.pallas import tpu as pltpu

F32 = jnp.float32
BF16 = jnp.bfloat16

D_MODEL = 2048
DEPTH = 4
N_META = 16
BLOCK = 128
META_START = BLOCK - N_META
EPS = 1e-6
M_HEADS = 8
M_DQK = 128
M_DV = 256
A_HEADS = 16
A_KV_HEADS = 4
A_GROUP = A_HEADS // A_KV_HEADS
A_DH = 128
WINDOW = 128
ROPE_THETA = 10000.0
P_HEADS = 8
P_NKEYS = 128
P_NEXP = P_NKEYS * P_NKEYS
P_DQ = 256
P_TOPK = 16

M_QK = M_HEADS * M_DQK
M_V = M_HEADS * M_DV
A_Q = A_HEADS * A_DH
A_KV = A_KV_HEADS * A_DH
OFF_GM = 0
OFF_MQ = OFF_GM + 2 * D_MODEL
OFF_MK = OFF_MQ + M_QK
OFF_MV = OFF_MK + M_QK
OFF_MO = OFF_MV + M_V
OFF_AQ = OFF_MO + M_V
OFF_AK = OFF_AQ + A_Q
OFF_AV = OFF_AK + A_KV
N_MAIN = OFF_AV + A_KV
N_GATE = 4 * M_HEADS
LANES = 128
STATE_W = M_DV + LANES
VMEM_LIMIT = 56 * 1024 * 1024


def _pick_tile(n, cap, mult):
    best = None
    for t in range(mult, min(n, cap) + 1, mult):
        if n % t == 0:
            best = t
    assert best is not None, (n, cap, mult)
    return best


def _params(sem):
    return pltpu.CompilerParams(dimension_semantics=sem, vmem_limit_bytes=VMEM_LIMIT)


def _norm_proj_kernel(h_ref, g_ref, w_ref, o_ref, xn_ref):
    @pl.when(pl.program_id(1) == 0)
    def _():
        x = h_ref[...]
        ms = jnp.mean(x * x, axis=-1, keepdims=True)
        xn_ref[...] = (x * lax.rsqrt(ms + EPS) * g_ref[...]).astype(BF16)

    o_ref[...] = jnp.dot(xn_ref[...], w_ref[...], preferred_element_type=F32).astype(o_ref.dtype)


def _column_tiles(w, tn):
    K, N = w.shape
    return w.reshape(K, N // tn, tn).transpose(1, 0, 2)


def _norm_proj(h, g, w_tiles, out_dtype):
    T, D = h.shape
    nj, _, tn = w_tiles.shape
    tm = _pick_tile(T, 1024, 256)
    return pl.pallas_call(
        _norm_proj_kernel,
        grid=(T // tm, nj),
        in_specs=[pl.BlockSpec((tm, D), lambda i, j: (i, 0)),
                  pl.BlockSpec((1, D), lambda i, j: (0, 0)),
                  pl.BlockSpec((None, D, tn), lambda i, j: (j, 0, 0))],
        out_specs=pl.BlockSpec((tm, tn), lambda i, j: (i, j)),
        out_shape=jax.ShapeDtypeStruct((T, nj * tn), out_dtype),
        scratch_shapes=[pltpu.VMEM((tm, D), BF16)],
        compiler_params=_params(("parallel", "arbitrary")),
        name="norm_proj",
    )(h, g, w_tiles)


def _mlstm_direction(q_ref, k_ref, v_ref, g_ref, bias_ref, o_ref, s_ref, m_ref, tok0, fwd):
    T = BLOCK
    g = g_ref[...] + bias_ref[...]
    rows = lax.broadcasted_iota(jnp.int32, (T, LANES), 0)
    valid = (tok0 + rows) >= META_START
    logsig = jnp.minimum(g, 0.0) - jnp.log1p(jnp.exp(-jnp.abs(g)))
    lf = jnp.where(valid, logsig, 0.0)
    li = jnp.where(valid, g, -jnp.inf)
    r = lax.broadcasted_iota(jnp.int32, (T, T), 0)
    c = lax.broadcasted_iota(jnp.int32, (T, T), 1)
    tri = (c <= r) if fwd else (c >= r)
    bcum = jnp.dot(tri.astype(F32), lf, preferred_element_type=F32,
                   precision=lax.Precision.HIGHEST)
    bcum_t = bcum.T
    li_t = li.T
    i_off = 0 if fwd else 2 * M_HEADS
    f_off = i_off + M_HEADS
    last = T - 1 if fwd else 0
    ones = jnp.ones((T, LANES), BF16)
    for h in range(M_HEADS):
        b_col = bcum[:, f_off + h:f_off + h + 1]
        b_row = bcum_t[f_off + h:f_off + h + 1, :]
        li_col = li[:, i_off + h:i_off + h + 1]
        li_row = li_t[i_off + h:i_off + h + 1, :]
        b_last = bcum[last:last + 1, f_off + h:f_off + h + 1]
        m_old = m_ref[h:h + 1, 0:1]
        qb = q_ref[:, h * M_DQK:(h + 1) * M_DQK].astype(BF16)
        kb = (k_ref[:, h * M_DQK:(h + 1) * M_DQK] * (M_DQK ** -0.5)).astype(BF16)
        v_ext = jnp.concatenate([v_ref[:, h * M_DV:(h + 1) * M_DV], ones], axis=1)
        state = s_ref[h]

        dm = jnp.where(tri, b_col - b_row + li_row, -jnp.inf)
        inter = b_col + m_old
        m_t = jnp.maximum(inter, jnp.max(dm, axis=1, keepdims=True))
        s_qk = lax.dot_general(qb, kb, (((1,), (1,)), ((), ())), preferred_element_type=F32)
        p = (jnp.exp(dm - m_t) * s_qk).astype(BF16)
        w_inter = jnp.exp(inter - m_t)
        lhs = jnp.concatenate([p, (w_inter * qb).astype(BF16)], axis=1)
        rhs = jnp.concatenate([v_ext, state.astype(BF16)], axis=0)
        nd = jnp.dot(lhs, rhs, preferred_element_type=F32)
        den = jnp.maximum(jnp.abs(nd[:, M_DV:]), jnp.exp(-m_t))
        inv = 1.0 / den
        o_ref[:, h * M_DV:(h + 1) * M_DV] = (
            nd[:, :M_DV] * jnp.concatenate([inv, inv], axis=1)).astype(o_ref.dtype)

        gk = b_last - b_col + li_col
        m_new = jnp.maximum(b_last + m_old, jnp.max(gk, axis=0, keepdims=True))
        wk = jnp.exp(gk - m_new)
        decay = jnp.exp(b_last + m_old - m_new)
        upd = lax.dot_general(kb, (wk * v_ext).astype(BF16), (((0,), (0,)), ((), ())),
                              preferred_element_type=F32)
        s_ref[h] = decay * state + upd
        m_ref[h:h + 1, :] = jnp.broadcast_to(m_new, (1, LANES))


def _mlstm_kernel(qf, kf, vf, gf, qb, kb, vb, gb, bias, of, ob, sf, sb, mf, mb):
    c = pl.program_id(1)
    nc = pl.num_programs(1)

    @pl.when(c == 0)
    def _():
        sf[...] = jnp.zeros_like(sf)
        sb[...] = jnp.zeros_like(sb)
        mf[...] = jnp.zeros_like(mf)
        mb[...] = jnp.zeros_like(mb)

    _mlstm_direction(qf, kf, vf, gf, bias, of, sf, mf, c * BLOCK, True)
    _mlstm_direction(qb, kb, vb, gb, bias, ob, sb, mb, (nc - 1 - c) * BLOCK, False)


def _mlstm(proj, gates, bias, B, nc):
    T = proj.shape[0]

    def fwd_map(col):
        return lambda b, c: (b * nc + c, col)

    def bwd_map(col):
        return lambda b, c: (b * nc + nc - 1 - c, col)

    def specs(mk):
        return [pl.BlockSpec((BLOCK, M_QK), mk(OFF_MQ // M_QK)),
                pl.BlockSpec((BLOCK, M_QK), mk(OFF_MK // M_QK)),
                pl.BlockSpec((BLOCK, M_V), mk(OFF_MV // M_V))]

    gate_f = pl.BlockSpec((BLOCK, LANES), fwd_map(0))
    gate_b = pl.BlockSpec((BLOCK, LANES), bwd_map(0))
    out = jax.ShapeDtypeStruct((T, M_V), BF16)
    return pl.pallas_call(
        _mlstm_kernel,
        grid=(B, nc),
        in_specs=(specs(fwd_map) + [gate_f] + specs(bwd_map) + [gate_b]
                  + [pl.BlockSpec((1, LANES), lambda b, c: (0, 0))]),
        out_specs=[pl.BlockSpec((BLOCK, M_V), fwd_map(0)), pl.BlockSpec((BLOCK, M_V), bwd_map(0))],
        out_shape=[out, out],
        scratch_shapes=[pltpu.VMEM((M_HEADS, M_DQK, STATE_W), F32),
                        pltpu.VMEM((M_HEADS, M_DQK, STATE_W), F32),
                        pltpu.VMEM((M_HEADS, LANES), F32),
                        pltpu.VMEM((M_HEADS, LANES), F32)],
        compiler_params=_params(("parallel", "arbitrary")),
        name="mlstm",
    )(proj, proj, proj, gates, proj, proj, proj, gates, bias)


def _qk_prep_kernel(q_ref, k_ref, v_ref, cos_ref, sin_ref, gq_ref, gk_ref, qo_ref, ko_ref, vo_ref):
    cos = cos_ref[...]
    sin = sin_ref[...]

    def norm_rope(x, g):
        xn = x * lax.rsqrt(jnp.mean(x * x, axis=-1, keepdims=True) + EPS) * g
        return xn * cos + pltpu.roll(xn, A_DH // 2, axis=1) * sin

    for h in range(A_HEADS):
        sl = slice(h * A_DH, (h + 1) * A_DH)
        qo_ref[:, sl] = (norm_rope(q_ref[:, sl].astype(F32), gq_ref[...]) * (A_DH ** -0.5)).astype(BF16)
    for h in range(A_KV_HEADS):
        sl = slice(h * A_DH, (h + 1) * A_DH)
        ko_ref[:, sl] = norm_rope(k_ref[:, sl].astype(F32), gk_ref[...]).astype(BF16)
    vo_ref[...] = v_ref[...].astype(BF16)


def _qk_prep(proj, cos, sin, g_q, g_k):
    T = proj.shape[0]
    tm = _pick_tile(T, 512, 128)
    row = lambda col: (lambda i: (i, col))
    const = lambda i: (0, 0)
    return pl.pallas_call(
        _qk_prep_kernel,
        grid=(T // tm,),
        in_specs=[pl.BlockSpec((tm, A_Q), row(OFF_AQ // A_Q)),
                  pl.BlockSpec((tm, A_KV), row(OFF_AK // A_KV)),
                  pl.BlockSpec((tm, A_KV), row(OFF_AV // A_KV)),
                  pl.BlockSpec((tm, A_DH), row(0)),
                  pl.BlockSpec((tm, A_DH), row(0)),
                  pl.BlockSpec((1, A_DH), const),
                  pl.BlockSpec((1, A_DH), const)],
        out_specs=[pl.BlockSpec((tm, A_Q), row(0)),
                   pl.BlockSpec((tm, A_KV), row(0)),
                   pl.BlockSpec((tm, A_KV), row(0))],
        out_shape=[jax.ShapeDtypeStruct((T, A_Q), BF16),
                   jax.ShapeDtypeStruct((T, A_KV), BF16),
                   jax.ShapeDtypeStruct((T, A_KV), BF16)],
        compiler_params=_params(("parallel",)),
        name="qk_prep",
    )(proj, proj, proj, cos, sin, g_q, g_k)


def _attn_kernel(sink_ref, q_ref, kp, kc, kn, km, vp, vc, vn, vm, o_ref):
    n = pl.program_id(1)
    nb = pl.num_programs(1)
    R = A_GROUP * BLOCK
    W = 4 * BLOCK
    row = lax.broadcasted_iota(jnp.int32, (R, W), 0)
    col = lax.broadcasted_iota(jnp.int32, (R, W), 1)
    t = row % BLOCK
    seg = col // BLOCK
    off = col % BLOCK
    dist = (1 - seg) * BLOCK + (t - off)
    blk = n - 1 + seg
    win_ok = (seg < 3) & (jnp.abs(dist) <= WINDOW) & (blk >= 1) & (blk <= nb - 1)
    meta_ok = (seg == 3) & (off >= META_START)
    mask = win_ok | meta_ok
    head_row = lax.broadcasted_iota(jnp.int32, (R, 1), 0) // BLOCK
    for j in range(A_KV_HEADS):
        sl = slice(j * A_DH, (j + 1) * A_DH)
        q4 = jnp.concatenate([q_ref[:, (j * A_GROUP + g) * A_DH:(j * A_GROUP + g + 1) * A_DH]
                              for g in range(A_GROUP)], axis=0)
        kcat = jnp.concatenate([kp[:, sl], kc[:, sl], kn[:, sl], km[:, sl]], axis=0)
        vcat = jnp.concatenate([vp[:, sl], vc[:, sl], vn[:, sl], vm[:, sl]], axis=0)
        s = lax.dot_general(q4, kcat, (((1,), (1,)), ((), ())), preferred_element_type=F32)
        s = jnp.where(mask, s, -jnp.inf)
        sink = jnp.zeros((R, 1), F32)
        for g in range(A_GROUP):
            sink = jnp.where(head_row == g, sink_ref[j * A_GROUP + g], sink)
        m = jnp.maximum(jnp.max(s, axis=1, keepdims=True), sink)
        p = jnp.exp(s - m)
        denom = jnp.sum(p, axis=1, keepdims=True) + jnp.exp(sink - m)
        o = jnp.dot(p.astype(BF16), vcat, preferred_element_type=F32) / denom
        for g in range(A_GROUP):
            hh = j * A_GROUP + g
            o_ref[:, hh * A_DH:(hh + 1) * A_DH] = o[g * BLOCK:(g + 1) * BLOCK].astype(BF16)


def _attention(q, k, v, sink, B, nb):
    T = q.shape[0]
    cur = lambda b, n, s: (b * nb + n, 0)
    prev = lambda b, n, s: (b * nb + jnp.maximum(n - 1, 0), 0)
    nxt = lambda b, n, s: (b * nb + jnp.minimum(n + 1, nb - 1), 0)
    meta = lambda b, n, s: (b * nb, 0)
    kv = lambda m: pl.BlockSpec((BLOCK, A_KV), m)
    grid_spec = pltpu.PrefetchScalarGridSpec(
        num_scalar_prefetch=1,
        grid=(B, nb),
        in_specs=[pl.BlockSpec((BLOCK, A_Q), cur),
                  kv(prev), kv(cur), kv(nxt), kv(meta), kv(prev), kv(cur), kv(nxt), kv(meta)],
        out_specs=pl.BlockSpec((BLOCK, A_Q), cur))
    return pl.pallas_call(
        _attn_kernel,
        grid_spec=grid_spec,
        out_shape=jax.ShapeDtypeStruct((T, A_Q), BF16),
        compiler_params=_params(("parallel", "arbitrary")),
        name="band_attn",
    )(sink, q, k, k, k, k, v, v, v, v)


MERGE_TN = 512


def _merge_kernel(hf_ref, hb_ref, mo_ref, gh_ref, oa_ref, gm_ref, b_ref, wm_ref, wa_ref, z_ref, hm_ref):
    for h in range(M_HEADS):
        sl = slice(h * M_DV, (h + 1) * M_DV)
        x = hf_ref[:, sl].astype(F32) + hb_ref[:, sl].astype(F32)
        xn = x * lax.rsqrt(jnp.mean(x * x, axis=-1, keepdims=True) + EPS) * gh_ref[:, sl]
        hm_ref[:, sl] = (xn * jax.nn.sigmoid(mo_ref[:, sl].astype(F32))).astype(BF16)
    for j in range(D_MODEL // MERGE_TN):
        cm = slice(j * MERGE_TN, (j + 1) * MERGE_TN)
        ca = slice(D_MODEL + j * MERGE_TN, D_MODEL + (j + 1) * MERGE_TN)
        y_m = jnp.dot(hm_ref[...], wm_ref[:, cm], preferred_element_type=F32)
        y_a = jnp.dot(oa_ref[...], wa_ref[:, cm], preferred_element_type=F32)
        gate_m = jax.nn.sigmoid(gm_ref[:, cm].astype(F32) + b_ref[:, cm])
        gate_a = jax.nn.sigmoid(gm_ref[:, ca].astype(F32) + b_ref[:, ca])
        z_ref[:, cm] = (gate_m * y_m + gate_a * y_a).astype(BF16)


def _merge(hf, hb, proj, g_mhead, oa, b_merge, wm, wa):
    T = hf.shape[0]
    tm = _pick_tile(T, 512, 128)
    row = lambda col: (lambda i: (i, col))
    const = lambda i: (0, 0)
    once = pl.Buffered(1)
    return pl.pallas_call(
        _merge_kernel,
        grid=(T // tm,),
        in_specs=[pl.BlockSpec((tm, M_V), row(0)),
                  pl.BlockSpec((tm, M_V), row(0)),
                  pl.BlockSpec((tm, M_V), row(OFF_MO // M_V)),
                  pl.BlockSpec((1, M_V), const),
                  pl.BlockSpec((tm, A_Q), row(0)),
                  pl.BlockSpec((tm, 2 * D_MODEL), row(OFF_GM // (2 * D_MODEL))),
                  pl.BlockSpec((1, 2 * D_MODEL), const),
                  pl.BlockSpec((M_V, D_MODEL), const, pipeline_mode=once),
                  pl.BlockSpec((A_Q, D_MODEL), const, pipeline_mode=once)],
        out_specs=pl.BlockSpec((tm, D_MODEL), row(0)),
        out_shape=jax.ShapeDtypeStruct((T, D_MODEL), BF16),
        scratch_shapes=[pltpu.VMEM((tm, M_V), BF16)],
        compiler_params=_params(("parallel",)),
        name="merge",
    )(hf, hb, proj, g_mhead, oa, proj, b_merge, wm, wa)


def _out_proj_kernel(z_ref, w_ref, h_ref, o_ref):
    for j in range(D_MODEL // MERGE_TN):
        cs = slice(j * MERGE_TN, (j + 1) * MERGE_TN)
        o_ref[:, cs] = h_ref[:, cs] + jnp.dot(z_ref[...], w_ref[:, cs], preferred_element_type=F32)


def _out_proj(z, w, h):
    T = z.shape[0]
    tm = _pick_tile(T, 1024, 256)
    row = lambda i: (i, 0)
    return pl.pallas_call(
        _out_proj_kernel,
        grid=(T // tm,),
        in_specs=[pl.BlockSpec((tm, D_MODEL), row),
                  pl.BlockSpec((D_MODEL, D_MODEL), lambda i: (0, 0), pipeline_mode=pl.Buffered(1)),
                  pl.BlockSpec((tm, D_MODEL), row)],
        out_specs=pl.BlockSpec((tm, D_MODEL), row),
        out_shape=jax.ShapeDtypeStruct((T, D_MODEL), F32),
        compiler_params=_params(("parallel",)),
        name="out_proj",
    )(z, w, h)


def _top_ranked(s, count):
    vals = []
    rank = jnp.full(s.shape, float(count), F32)
    for i in range(count):
        m = jnp.max(s, axis=0, keepdims=True)
        vals.append(m)
        hit = s >= m
        rank = jnp.where(hit, float(i), rank)
        s = jnp.where(hit, -jnp.inf, s)
    return vals, rank


def _top_values(s, count):
    m = jnp.max(s, axis=0, keepdims=True)
    vals = [m]
    for _ in range(count - 1):
        m = jnp.max(jnp.where(s < m, s, -jnp.inf), axis=0, keepdims=True)
        vals.append(m)
    return vals


def _peer_score_kernel(h_ref, g_ref, wq_ref, k1_ref, k2_ref, xt_ref, n_ref, c_ref, r2_ref, e2_ref):
    x = h_ref[...]
    x = x * lax.rsqrt(jnp.mean(x * x, axis=-1, keepdims=True) + EPS) * g_ref[...]
    xt = x.T.astype(BF16)
    xt_ref[...] = xt
    qt = jnp.dot(wq_ref[...], xt, preferred_element_type=F32)
    half = P_DQ // 2
    for h in range(P_HEADS):
        q1 = qt[h * P_DQ:h * P_DQ + half].astype(BF16)
        q2 = qt[h * P_DQ + half:(h + 1) * P_DQ].astype(BF16)
        s1 = jnp.dot(k1_ref[h], q1, preferred_element_type=F32)
        s2 = jnp.dot(k2_ref[h], q2, preferred_element_type=F32)
        v1 = _top_values(s1, P_TOPK)
        v2, r2 = _top_ranked(s2, P_TOPK)
        rows = [[v1[i] + v2[j] for j in range(P_TOPK // (i + 1))] for i in range(P_TOPK)]
        cand = jnp.concatenate([c for row in rows for c in row], axis=0)
        s = cand
        for _ in range(P_TOPK):
            c16 = jnp.max(s, axis=0, keepdims=True)
            s = jnp.where(s >= c16, -jnp.inf, s)
        cmax = v1[0] + v2[0]
        z = jnp.sum(jnp.where(cand >= c16, jnp.exp(cand - cmax), 0.0), axis=0, keepdims=True)
        n_sel = jnp.zeros_like(s1)
        for j in range(P_TOPK):
            n_sel = jnp.where(s1 + v2[j] >= c16, float(j + 1), n_sel)
        sl = slice(h * P_NKEYS, (h + 1) * P_NKEYS)
        n_ref[sl, :] = n_sel
        c_ref[sl, :] = jnp.exp(s1 - v1[0]) / z
        r2_ref[sl, :] = r2.astype(BF16)
        e2_ref[sl, :] = jnp.exp(s2 - v2[0]).astype(BF16)


def _peer_scores(h, g, wq_t, k1, k2):
    T = h.shape[0]
    tm = _pick_tile(T, 256, 128)
    HK = P_HEADS * P_NKEYS
    tok = lambda i: (0, i)
    f32_side = jax.ShapeDtypeStruct((HK, T), F32)
    bf16_side = jax.ShapeDtypeStruct((HK, T), BF16)
    return pl.pallas_call(
        _peer_score_kernel,
        grid=(T // tm,),
        in_specs=[pl.BlockSpec((tm, D_MODEL), lambda i: (i, 0)),
                  pl.BlockSpec((1, D_MODEL), lambda i: (0, 0)),
                  pl.BlockSpec((P_HEADS * P_DQ, D_MODEL), lambda i: (0, 0)),
                  pl.BlockSpec((P_HEADS, P_NKEYS, P_DQ // 2), lambda i: (0, 0, 0)),
                  pl.BlockSpec((P_HEADS, P_NKEYS, P_DQ // 2), lambda i: (0, 0, 0))],
        out_specs=[pl.BlockSpec((D_MODEL, tm), tok)] + [pl.BlockSpec((HK, tm), tok)] * 4,
        out_shape=[jax.ShapeDtypeStruct((D_MODEL, T), BF16), f32_side, f32_side, bf16_side, bf16_side],
        compiler_params=_params(("parallel",)),
        name="peer_scores",
    )(h, g, wq_t, k1, k2)


PEER_TE = 1024
PEER_NJ = P_NEXP // PEER_TE
PEER_DROWS = 512


def _peer_expert_kernel(xt_ref, n_ref, c_ref, r2_ref, e2_ref, u_ref, vt_ref, o_ref, w0_ref, w1_ref, a_ref):
    j = pl.program_id(1)
    nj = pl.num_programs(1) - 1
    na = PEER_TE // P_NKEYS
    w_ref = (w0_ref, w1_ref)

    def gate_block(slot, al):
        a = j * na + al
        gate = None
        for h in range(P_HEADS):
            n_sel = n_ref[pl.ds(h * P_NKEYS + a, 1), :].astype(BF16)
            coef = c_ref[pl.ds(h * P_NKEYS + a, 1), :].astype(BF16)
            sl = slice(h * P_NKEYS, (h + 1) * P_NKEYS)
            term = jnp.where(r2_ref[sl, :] < n_sel, e2_ref[sl, :], 0.0) * coef
            gate = term if gate is None else gate + term
        w_ref[slot][al * P_NKEYS:(al + 1) * P_NKEYS, :] = gate

    def activate_block(slot, al):
        rows = slice(al * P_NKEYS, (al + 1) * P_NKEYS)
        x = a_ref[rows, :]
        act = 0.5 * x * (1.0 + lax.erf(x * (2.0 ** -0.5)))
        w_ref[slot][rows, :] = w_ref[slot][rows, :] * act

    def accumulate(slot, k):
        rows = slice(k * PEER_DROWS, (k + 1) * PEER_DROWS)
        o_ref[rows, :] += jnp.dot(vt_ref[rows, :], w_ref[slot][...], preferred_element_type=F32)

    def stage_a(slot):
        for al in range(na):
            gate_block(slot, al)
        half = PEER_TE // 2
        for r in range(2):
            rows = slice(r * half, (r + 1) * half)
            a_ref[rows, :] = jnp.dot(u_ref[rows, :], xt_ref[...],
                                     preferred_element_type=F32).astype(BF16)
        for al in range(na):
            activate_block(slot, al)

    def stage_b(slot):
        for k in range(D_MODEL // PEER_DROWS):
            accumulate(slot, k)

    @pl.when(j == 0)
    def _():
        o_ref[...] = jnp.zeros_like(o_ref)
        stage_a(0)

    for parity in range(2):
        @pl.when((j > 0) & (j < nj) & (j % 2 == parity))
        def _():
            stage_a(parity)
            stage_b(1 - parity)

    @pl.when(j == nj)
    def _():
        stage_b((PEER_NJ - 1) % 2)


def _peer_experts(xt, n_sel, coef, r2, e2, u, vt):
    T = xt.shape[1]
    tm = _pick_tile(T, 768, 256)
    te = PEER_TE
    nj = PEER_NJ
    HK = P_HEADS * P_NKEYS
    once = pl.Buffered(1) if tm > 512 else None
    tok = lambda i, j: (0, i)
    return pl.pallas_call(
        _peer_expert_kernel,
        grid=(T // tm, nj + 1),
        in_specs=[pl.BlockSpec((D_MODEL, tm), tok, pipeline_mode=once)]
                 + [pl.BlockSpec((HK, tm), tok, pipeline_mode=once)] * 4
                 + [pl.BlockSpec((te, D_MODEL), lambda i, j: (jnp.minimum(j, nj - 1), 0)),
                    pl.BlockSpec((None, D_MODEL, te), lambda i, j: (jnp.maximum(j - 1, 0), 0, 0))],
        out_specs=pl.BlockSpec((D_MODEL, tm), tok),
        out_shape=jax.ShapeDtypeStruct((D_MODEL, T), F32),
        scratch_shapes=[pltpu.VMEM((te, tm), BF16), pltpu.VMEM((te, tm), BF16), pltpu.VMEM((te, tm), BF16)],
        compiler_params=_params(("parallel", "arbitrary")),
        name="peer_experts",
    )(xt, n_sel, coef, r2, e2, u, vt)


def _add_transposed_kernel(h_ref, yt_ref, o_ref):
    o_ref[...] = h_ref[...] + yt_ref[...].T


def _add_transposed(h, yt):
    T, D = h.shape
    tm = _pick_tile(T, 512, 128)
    return pl.pallas_call(
        _add_transposed_kernel,
        grid=(T // tm,),
        in_specs=[pl.BlockSpec((tm, D), lambda i: (i, 0)), pl.BlockSpec((D, tm), lambda i: (0, i))],
        out_specs=pl.BlockSpec((tm, D), lambda i: (i, 0)),
        out_shape=jax.ShapeDtypeStruct((T, D), F32),
        compiler_params=_params(("parallel",)),
        name="add_transposed",
    )(h, yt)


def _prep_layer(l, g_mix, w_in, b_mgate, g_mhead, g_q, g_k, attn_sink, w_branch_m, w_branch_a,
                b_merge, w_out, g_ffn, w_pq, sub_k1, sub_k2, u_tab, v_tab):
    n_gate0 = 2 * M_QK + 2 * M_V
    n_gm0 = n_gate0 + N_GATE + A_Q + 2 * A_KV
    w = w_in[l]
    w_main = jnp.concatenate([w[:, n_gm0:], w[:, :n_gate0], w[:, n_gate0 + N_GATE:n_gm0]], axis=1).astype(BF16)
    w_gate = jnp.pad(w[:, n_gate0:n_gate0 + N_GATE], ((0, 0), (0, LANES - N_GATE))).astype(BF16)
    bias = jnp.pad(b_mgate[l], (0, LANES - N_GATE)).reshape(1, LANES)
    return dict(
        g_mix=g_mix[l].reshape(1, D_MODEL), w_main=_column_tiles(w_main, 1024),
        w_gate=_column_tiles(w_gate, LANES), bias=bias,
        g_mhead=g_mhead[l].reshape(1, M_V), g_q=g_q[l].reshape(1, A_DH), g_k=g_k[l].reshape(1, A_DH),
        sink=attn_sink[l], wm=w_branch_m[l].astype(BF16), wa=w_branch_a[l].astype(BF16),
        b_merge=b_merge[l].reshape(1, 2 * D_MODEL), wo=w_out[l].astype(BF16),
        g_ffn=g_ffn[l].reshape(1, D_MODEL), wq_t=w_pq[l].T.astype(BF16),
        k1=sub_k1[l].astype(BF16), k2=sub_k2[l].astype(BF16),
        u=u_tab[l].astype(BF16),
        vt=v_tab[l].astype(BF16).reshape(PEER_NJ, PEER_TE, D_MODEL).transpose(0, 2, 1))


def _layer(h, p, cos, sin, B, nc):
    proj = _norm_proj(h, p["g_mix"], p["w_main"], BF16)
    gates = _norm_proj(h, p["g_mix"], p["w_gate"], F32)
    hf, hb = _mlstm(proj, gates, p["bias"], B, nc)
    q, k, v = _qk_prep(proj, cos, sin, p["g_q"], p["g_k"])
    oa = _attention(q, k, v, p["sink"], B, nc)
    z = _merge(hf, hb, proj, p["g_mhead"], oa, p["b_merge"], p["wm"], p["wa"])
    h = _out_proj(z, p["wo"], h)
    xt, n_sel, coef, r2, e2 = _peer_scores(h, p["g_ffn"], p["wq_t"], p["k1"], p["k2"])
    return _add_transposed(h, _peer_experts(xt, n_sel, coef, r2, e2, p["u"], p["vt"]))


def _trunk(x, meta_tokens, layers):
    B, L, D = x.shape
    Lp = BLOCK + L
    nc = Lp // BLOCK
    meta = jnp.broadcast_to(meta_tokens.astype(x.dtype)[None], (B, N_META, D))
    h = jnp.concatenate([jnp.zeros((B, META_START, D), x.dtype), meta, x], axis=1).reshape(B * Lp, D)
    pos = (jnp.arange(Lp) - META_START).astype(F32)
    half = A_DH // 2
    inv = ROPE_THETA ** (-jnp.arange(half, dtype=F32) / half)
    ang = pos[:, None] * inv[None, :]
    cos = jnp.tile(jnp.concatenate([jnp.cos(ang), jnp.cos(ang)], axis=1), (B, 1))
    sin = jnp.tile(jnp.concatenate([-jnp.sin(ang), jnp.sin(ang)], axis=1), (B, 1))
    for p in layers:
        h = _layer(h, p, cos, sin, B, nc)
    return h.reshape(B, Lp, D)[:, BLOCK:]


def kernel(x_prompt, x_sample, meta_tokens, g_mix, w_in, b_mgate, g_mhead, g_q, g_k, attn_sink,
           w_branch_m, w_branch_a, b_merge, w_out, g_ffn, w_pq, sub_k1, sub_k2, u_tab, v_tab):
    weights = (g_mix, w_in, b_mgate, g_mhead, g_q, g_k, attn_sink, w_branch_m, w_branch_a,
               b_merge, w_out, g_ffn, w_pq, sub_k1, sub_k2, u_tab, v_tab)
    layers = [_prep_layer(l, *weights) for l in range(w_in.shape[0])]
    return (_trunk(x_prompt, meta_tokens, layers), _trunk(x_sample, meta_tokens, layers))
```

```python
import functools

import jax
import jax.numpy as jnp
from jax import lax
from jax.experimental import pallas as pl
from jax.experimental.pallas import tpu as pltpu

F32 = jnp.float32
BF16 = jnp.bfloat16

D_MODEL = 2048
DEPTH = 4
N_META = 16
BLOCK = 128
META_START = BLOCK - N_META
EPS = 1e-6
M_HEADS = 8
M_DQK = 128
M_DV = 256
A_HEADS = 16
A_KV_HEADS = 4
A_GROUP = A_HEADS // A_KV_HEADS
A_DH = 128
WINDOW = 128
ROPE_THETA = 10000.0
P_HEADS = 8
P_NKEYS = 128
P_NEXP = P_NKEYS * P_NKEYS
P_DQ = 256
P_TOPK = 16

M_QK = M_HEADS * M_DQK
M_V = M_HEADS * M_DV
A_Q = A_HEADS * A_DH
A_KV = A_KV_HEADS * A_DH
OFF_GM = 0
OFF_MQ = OFF_GM + 2 * D_MODEL
OFF_MK = OFF_MQ + M_QK
OFF_MV = OFF_MK + M_QK
OFF_MO = OFF_MV + M_V
OFF_AQ = OFF_MO + M_V
OFF_AK = OFF_AQ + A_Q
OFF_AV = OFF_AK + A_KV
N_MAIN = OFF_AV + A_KV
N_GATE = 4 * M_HEADS
LANES = 128
STATE_W = M_DV + LANES
VMEM_LIMIT = 56 * 1024 * 1024


def _pick_tile(n, cap, mult):
    best = None
    for t in range(mult, min(n, cap) + 1, mult):
        if n % t == 0:
            best = t
    assert best is not None, (n, cap, mult)
    return best


def _params(sem):
    return pltpu.CompilerParams(dimension_semantics=sem, vmem_limit_bytes=VMEM_LIMIT)


def _norm_proj_kernel(h_ref, g_ref, w_ref, o_ref, xn_ref):
    @pl.when(pl.program_id(1) == 0)
    def _():
        x = h_ref[...]
        ms = jnp.mean(x * x, axis=-1, keepdims=True)
        xn_ref[...] = (x * lax.rsqrt(ms + EPS) * g_ref[...]).astype(BF16)

    o_ref[...] = jnp.dot(xn_ref[...], w_ref[...], preferred_element_type=F32).astype(o_ref.dtype)


def _column_tiles(w, tn):
    K, N = w.shape
    return w.reshape(K, N // tn, tn).transpose(1, 0, 2)


def _norm_proj(h, g, w_tiles, out_dtype):
    T, D = h.shape
    nj, _, tn = w_tiles.shape
    tm = _pick_tile(T, 1024, 256)
    return pl.pallas_call(
        _norm_proj_kernel,
        grid=(T // tm, nj),
        in_specs=[pl.BlockSpec((tm, D), lambda i, j: (i, 0)),
                  pl.BlockSpec((1, D), lambda i, j: (0, 0)),
                  pl.BlockSpec((None, D, tn), lambda i, j: (j, 0, 0))],
        out_specs=pl.BlockSpec((tm, tn), lambda i, j: (i, j)),
        out_shape=jax.ShapeDtypeStruct((T, nj * tn), out_dtype),
        scratch_shapes=[pltpu.VMEM((tm, D), BF16)],
        compiler_params=_params(("parallel", "arbitrary")),
        name="norm_proj",
    )(h, g, w_tiles)


def _mlstm_direction(q_ref, k_ref, v_ref, g_ref, bias_ref, o_ref, s_ref, m_ref, tok0, fwd):
    T = BLOCK
    g = g_ref[...] + bias_ref[...]
    rows = lax.broadcasted_iota(jnp.int32, (T, LANES), 0)
    valid = (tok0 + rows) >= META_START
    logsig = jnp.minimum(g, 0.0) - jnp.log1p(jnp.exp(-jnp.abs(g)))
    lf = jnp.where(valid, logsig, 0.0)
    li = jnp.where(valid, g, -jnp.inf)
    r = lax.broadcasted_iota(jnp.int32, (T, T), 0)
    c = lax.broadcasted_iota(jnp.int32, (T, T), 1)
    tri = (c <= r) if fwd else (c >= r)
    bcum = jnp.dot(tri.astype(F32), lf, preferred_element_type=F32,
                   precision=lax.Precision.HIGHEST)
    bcum_t = bcum.T
    li_t = li.T
    i_off = 0 if fwd else 2 * M_HEADS
    f_off = i_off + M_HEADS
    last = T - 1 if fwd else 0
    ones = jnp.ones((T, LANES), BF16)

    def decay_matrix(h):
        c = f_off + h
        return jnp.where(tri, bcum[:, c:c + 1] - bcum_t[c:c + 1, :] + li_t[i_off + h:i_off + h + 1, :],
                         -jnp.inf)

    lane = lax.broadcasted_iota(jnp.int32, (T, LANES), 1)
    mx_all = jnp.full((T, LANES), -jnp.inf, F32)
    for h in range(M_HEADS):
        mx_all = jnp.where(lane == f_off + h, jnp.max(decay_matrix(h), axis=1, keepdims=True), mx_all)
    li_al = pltpu.roll(li, M_HEADS, axis=1)
    m_row = m_ref[0:1, :]
    b_last_row = bcum[last:last + 1, :]
    inter_all = bcum + m_row
    m_t_all = jnp.maximum(inter_all, mx_all)
    w_inter_all = jnp.exp(inter_all - m_t_all)
    em_all = jnp.exp(-m_t_all)
    gk_all = b_last_row - bcum + li_al
    m_new_row = jnp.maximum(b_last_row + m_row, jnp.max(gk_all, axis=0, keepdims=True))
    wk_all = jnp.exp(gk_all - m_new_row)
    decay_row = jnp.exp(b_last_row + m_row - m_new_row)
    m_ref[0:1, :] = m_new_row

    for h in range(M_HEADS):
        c = f_off + h
        qb = q_ref[:, h * M_DQK:(h + 1) * M_DQK].astype(BF16)
        kb = (k_ref[:, h * M_DQK:(h + 1) * M_DQK] * (M_DQK ** -0.5)).astype(BF16)
        v_ext = jnp.concatenate([v_ref[:, h * M_DV:(h + 1) * M_DV], ones], axis=1)
        state = s_ref[h]

        m_t = m_t_all[:, c:c + 1]
        s_qk = lax.dot_general(qb, kb, (((1,), (1,)), ((), ())), preferred_element_type=F32)
        p = (jnp.exp(decay_matrix(h) - m_t) * s_qk).astype(BF16)
        w_inter = w_inter_all[:, c:c + 1]
        lhs = jnp.concatenate([p, (w_inter * qb).astype(BF16)], axis=1)
        rhs = jnp.concatenate([v_ext, state.astype(BF16)], axis=0)
        nd = jnp.dot(lhs, rhs, preferred_element_type=F32)
        den = jnp.maximum(jnp.abs(nd[:, M_DV:]), em_all[:, c:c + 1])
        inv = 1.0 / den
        o_ref[:, h * M_DV:(h + 1) * M_DV] = (
            nd[:, :M_DV] * jnp.concatenate([inv, inv], axis=1)).astype(o_ref.dtype)

        upd = lax.dot_general(kb, (wk_all[:, c:c + 1] * v_ext).astype(BF16), (((0,), (0,)), ((), ())),
                              preferred_element_type=F32)
        s_ref[h] = decay_row[:, c:c + 1] * state + upd


def _mlstm_kernel(qf, kf, vf, gf, qb, kb, vb, gb, bias, of, ob, sf, sb, mf, mb):
    c = pl.program_id(1)
    nc = pl.num_programs(1)

    @pl.when(c == 0)
    def _():
        sf[...] = jnp.zeros_like(sf)
        sb[...] = jnp.zeros_like(sb)
        mf[...] = jnp.zeros_like(mf)
        mb[...] = jnp.zeros_like(mb)

    _mlstm_direction(qf, kf, vf, gf, bias, of, sf, mf, c * BLOCK, True)
    _mlstm_direction(qb, kb, vb, gb, bias, ob, sb, mb, (nc - 1 - c) * BLOCK, False)


def _mlstm(proj, gates, bias, B, nc):
    T = proj.shape[0]

    def fwd_map(col):
        return lambda b, c: (b * nc + c, col)

    def bwd_map(col):
        return lambda b, c: (b * nc + nc - 1 - c, col)

    def specs(mk):
        return [pl.BlockSpec((BLOCK, M_QK), mk(OFF_MQ // M_QK)),
                pl.BlockSpec((BLOCK, M_QK), mk(OFF_MK // M_QK)),
                pl.BlockSpec((BLOCK, M_V), mk(OFF_MV // M_V))]

    gate_f = pl.BlockSpec((BLOCK, LANES), fwd_map(0))
    gate_b = pl.BlockSpec((BLOCK, LANES), bwd_map(0))
    out = jax.ShapeDtypeStruct((T, M_V), BF16)
    return pl.pallas_call(
        _mlstm_kernel,
        grid=(B, nc),
        in_specs=(specs(fwd_map) + [gate_f] + specs(bwd_map) + [gate_b]
                  + [pl.BlockSpec((1, LANES), lambda b, c: (0, 0))]),
        out_specs=[pl.BlockSpec((BLOCK, M_V), fwd_map(0)), pl.BlockSpec((BLOCK, M_V), bwd_map(0))],
        out_shape=[out, out],
        scratch_shapes=[pltpu.VMEM((M_HEADS, M_DQK, STATE_W), F32),
                        pltpu.VMEM((M_HEADS, M_DQK, STATE_W), F32),
                        pltpu.VMEM((M_HEADS, LANES), F32),
                        pltpu.VMEM((M_HEADS, LANES), F32)],
        compiler_params=_params(("parallel", "arbitrary")),
        name="mlstm",
    )(proj, proj, proj, gates, proj, proj, proj, gates, bias)


def _qk_prep_kernel(q_ref, k_ref, v_ref, cos_ref, sin_ref, gq_ref, gk_ref, qo_ref, ko_ref, vo_ref):
    cos = cos_ref[...]
    sin = sin_ref[...]

    def norm_rope(x, g):
        xn = x * lax.rsqrt(jnp.mean(x * x, axis=-1, keepdims=True) + EPS) * g
        return xn * cos + pltpu.roll(xn, A_DH // 2, axis=1) * sin

    for h in range(A_HEADS):
        sl = slice(h * A_DH, (h + 1) * A_DH)
        qo_ref[:, sl] = (norm_rope(q_ref[:, sl].astype(F32), gq_ref[...]) * (A_DH ** -0.5)).astype(BF16)
    for h in range(A_KV_HEADS):
        sl = slice(h * A_DH, (h + 1) * A_DH)
        ko_ref[:, sl] = norm_rope(k_ref[:, sl].astype(F32), gk_ref[...]).astype(BF16)
    vo_ref[...] = v_ref[...].astype(BF16)


def _qk_prep(proj, cos, sin, g_q, g_k):
    T = proj.shape[0]
    tm = _pick_tile(T, 512, 128)
    row = lambda col: (lambda i: (i, col))
    const = lambda i: (0, 0)
    return pl.pallas_call(
        _qk_prep_kernel,
        grid=(T // tm,),
        in_specs=[pl.BlockSpec((tm, A_Q), row(OFF_AQ // A_Q)),
                  pl.BlockSpec((tm, A_KV), row(OFF_AK // A_KV)),
                  pl.BlockSpec((tm, A_KV), row(OFF_AV // A_KV)),
                  pl.BlockSpec((tm, A_DH), row(0)),
                  pl.BlockSpec((tm, A_DH), row(0)),
                  pl.BlockSpec((1, A_DH), const),
                  pl.BlockSpec((1, A_DH), const)],
        out_specs=[pl.BlockSpec((tm, A_Q), row(0)),
                   pl.BlockSpec((tm, A_KV), row(0)),
                   pl.BlockSpec((tm, A_KV), row(0))],
        out_shape=[jax.ShapeDtypeStruct((T, A_Q), BF16),
                   jax.ShapeDtypeStruct((T, A_KV), BF16),
                   jax.ShapeDtypeStruct((T, A_KV), BF16)],
        compiler_params=_params(("parallel",)),
        name="qk_prep",
    )(proj, proj, proj, cos, sin, g_q, g_k)


def _attn_kernel(sink_ref, q_ref, kp, kc, kn, km, vp, vc, vn, vm, o_ref):
    n = pl.program_id(1)
    nb = pl.num_programs(1)
    R = A_GROUP * BLOCK
    W = 4 * BLOCK
    row = lax.broadcasted_iota(jnp.int32, (R, W), 0)
    col = lax.broadcasted_iota(jnp.int32, (R, W), 1)
    t = row % BLOCK
    seg = col // BLOCK
    off = col % BLOCK
    dist = (1 - seg) * BLOCK + (t - off)
    blk = n - 1 + seg
    win_ok = (seg < 3) & (jnp.abs(dist) <= WINDOW) & (blk >= 1) & (blk <= nb - 1)
    meta_ok = (seg == 3) & (off >= META_START)
    mask = win_ok | meta_ok
    head_row = lax.broadcasted_iota(jnp.int32, (R, 1), 0) // BLOCK
    for j in range(A_KV_HEADS):
        sl = slice(j * A_DH, (j + 1) * A_DH)
        q4 = jnp.concatenate([q_ref[:, (j * A_GROUP + g) * A_DH:(j * A_GROUP + g + 1) * A_DH]
                              for g in range(A_GROUP)], axis=0)
        kcat = jnp.concatenate([kp[:, sl], kc[:, sl], kn[:, sl], km[:, sl]], axis=0)
        vcat = jnp.concatenate([vp[:, sl], vc[:, sl], vn[:, sl], vm[:, sl]], axis=0)
        s = lax.dot_general(q4, kcat, (((1,), (1,)), ((), ())), preferred_element_type=F32)
        s = jnp.where(mask, s, -jnp.inf)
        sink = jnp.zeros((R, 1), F32)
        for g in range(A_GROUP):
            sink = jnp.where(head_row == g, sink_ref[j * A_GROUP + g], sink)
        m = jnp.maximum(jnp.max(s, axis=1, keepdims=True), sink)
        p = jnp.exp(s - m)
        denom = jnp.sum(p, axis=1, keepdims=True) + jnp.exp(sink - m)
        o = jnp.dot(p.astype(BF16), vcat, preferred_element_type=F32) / denom
        for g in range(A_GROUP):
            hh = j * A_GROUP + g
            o_ref[:, hh * A_DH:(hh + 1) * A_DH] = o[g * BLOCK:(g + 1) * BLOCK].astype(BF16)


def _attention(q, k, v, sink, B, nb):
    T = q.shape[0]
    cur = lambda b, n, s: (b * nb + n, 0)
    prev = lambda b, n, s: (b * nb + jnp.maximum(n - 1, 0), 0)
    nxt = lambda b, n, s: (b * nb + jnp.minimum(n + 1, nb - 1), 0)
    meta = lambda b, n, s: (b * nb, 0)
    kv = lambda m: pl.BlockSpec((BLOCK, A_KV), m)
    grid_spec = pltpu.PrefetchScalarGridSpec(
        num_scalar_prefetch=1,
        grid=(B, nb),
        in_specs=[pl.BlockSpec((BLOCK, A_Q), cur),
                  kv(prev), kv(cur), kv(nxt), kv(meta), kv(prev), kv(cur), kv(nxt), kv(meta)],
        out_specs=pl.BlockSpec((BLOCK, A_Q), cur))
    return pl.pallas_call(
        _attn_kernel,
        grid_spec=grid_spec,
        out_shape=jax.ShapeDtypeStruct((T, A_Q), BF16),
        compiler_params=_params(("parallel", "arbitrary")),
        name="band_attn",
    )(sink, q, k, k, k, k, v, v, v, v)


MERGE_TN = 512


def _merge_kernel(hf_ref, hb_ref, mo_ref, gh_ref, oa_ref, gm_ref, b_ref, wm_ref, wa_ref, z_ref, hm_ref):
    for h in range(M_HEADS):
        sl = slice(h * M_DV, (h + 1) * M_DV)
        x = hf_ref[:, sl].astype(F32) + hb_ref[:, sl].astype(F32)
        xn = x * lax.rsqrt(jnp.mean(x * x, axis=-1, keepdims=True) + EPS) * gh_ref[:, sl]
        hm_ref[:, sl] = (xn * jax.nn.sigmoid(mo_ref[:, sl].astype(F32))).astype(BF16)
    for j in range(D_MODEL // MERGE_TN):
        cm = slice(j * MERGE_TN, (j + 1) * MERGE_TN)
        ca = slice(D_MODEL + j * MERGE_TN, D_MODEL + (j + 1) * MERGE_TN)
        y_m = jnp.dot(hm_ref[...], wm_ref[:, cm], preferred_element_type=F32)
        y_a = jnp.dot(oa_ref[...], wa_ref[:, cm], preferred_element_type=F32)
        gate_m = jax.nn.sigmoid(gm_ref[:, cm].astype(F32) + b_ref[:, cm])
        gate_a = jax.nn.sigmoid(gm_ref[:, ca].astype(F32) + b_ref[:, ca])
        z_ref[:, cm] = (gate_m * y_m + gate_a * y_a).astype(BF16)


def _merge(hf, hb, proj, g_mhead, oa, b_merge, wm, wa):
    T = hf.shape[0]
    tm = _pick_tile(T, 512, 128)
    row = lambda col: (lambda i: (i, col))
    const = lambda i: (0, 0)
    once = pl.Buffered(1)
    return pl.pallas_call(
        _merge_kernel,
        grid=(T // tm,),
        in_specs=[pl.BlockSpec((tm, M_V), row(0)),
                  pl.BlockSpec((tm, M_V), row(0)),
                  pl.BlockSpec((tm, M_V), row(OFF_MO // M_V)),
                  pl.BlockSpec((1, M_V), const),
                  pl.BlockSpec((tm, A_Q), row(0)),
                  pl.BlockSpec((tm, 2 * D_MODEL), row(OFF_GM // (2 * D_MODEL))),
                  pl.BlockSpec((1, 2 * D_MODEL), const),
                  pl.BlockSpec((M_V, D_MODEL), const, pipeline_mode=once),
                  pl.BlockSpec((A_Q, D_MODEL), const, pipeline_mode=once)],
        out_specs=pl.BlockSpec((tm, D_MODEL), row(0)),
        out_shape=jax.ShapeDtypeStruct((T, D_MODEL), BF16),
        scratch_shapes=[pltpu.VMEM((tm, M_V), BF16)],
        compiler_params=_params(("parallel",)),
        name="merge",
    )(hf, hb, proj, g_mhead, oa, proj, b_merge, wm, wa)


def _out_proj_kernel(z_ref, w_ref, h_ref, o_ref):
    for j in range(D_MODEL // MERGE_TN):
        cs = slice(j * MERGE_TN, (j + 1) * MERGE_TN)
        o_ref[:, cs] = h_ref[:, cs] + jnp.dot(z_ref[...], w_ref[:, cs], preferred_element_type=F32)


def _out_proj(z, w, h):
    T = z.shape[0]
    tm = _pick_tile(T, 1024, 256)
    row = lambda i: (i, 0)
    return pl.pallas_call(
        _out_proj_kernel,
        grid=(T // tm,),
        in_specs=[pl.BlockSpec((tm, D_MODEL), row),
                  pl.BlockSpec((D_MODEL, D_MODEL), lambda i: (0, 0), pipeline_mode=pl.Buffered(1)),
                  pl.BlockSpec((tm, D_MODEL), row)],
        out_specs=pl.BlockSpec((tm, D_MODEL), row),
        out_shape=jax.ShapeDtypeStruct((T, D_MODEL), F32),
        compiler_params=_params(("parallel",)),
        name="out_proj",
    )(z, w, h)


def _top_ranked(s, count):
    vals = []
    rank = jnp.full(s.shape, float(count), F32)
    for i in range(count):
        m = jnp.max(s, axis=0, keepdims=True)
        vals.append(m)
        hit = s >= m
        rank = jnp.where(hit, float(i), rank)
        s = jnp.where(hit, -jnp.inf, s)
    return vals, rank


def _top_values(s, count):
    m = jnp.max(s, axis=0, keepdims=True)
    vals = [m]
    for _ in range(count - 1):
        m = jnp.max(jnp.where(s < m, s, -jnp.inf), axis=0, keepdims=True)
        vals.append(m)
    return vals


def _peer_score_kernel(h_ref, g_ref, wq_ref, k1_ref, k2_ref, xt_ref, n_ref, c_ref, r2_ref, e2_ref):
    x = h_ref[...]
    x = x * lax.rsqrt(jnp.mean(x * x, axis=-1, keepdims=True) + EPS) * g_ref[...]
    xt = x.T.astype(BF16)
    xt_ref[...] = xt
    qt = jnp.dot(wq_ref[...], xt, preferred_element_type=F32)
    half = P_DQ // 2
    for h in range(P_HEADS):
        q1 = qt[h * P_DQ:h * P_DQ + half].astype(BF16)
        q2 = qt[h * P_DQ + half:(h + 1) * P_DQ].astype(BF16)
        s1 = jnp.dot(k1_ref[h], q1, preferred_element_type=F32)
        s2 = jnp.dot(k2_ref[h], q2, preferred_element_type=F32)
        v1 = _top_values(s1, P_TOPK)
        v2, r2 = _top_ranked(s2, P_TOPK)
        rows = [[v1[i] + v2[j] for j in range(P_TOPK // (i + 1))] for i in range(P_TOPK)]
        cand = jnp.concatenate([c for row in rows for c in row], axis=0)
        s = cand
        for _ in range(P_TOPK):
            c16 = jnp.max(s, axis=0, keepdims=True)
            s = jnp.where(s >= c16, -jnp.inf, s)
        cmax = v1[0] + v2[0]
        z = jnp.sum(jnp.where(cand >= c16, jnp.exp(cand - cmax), 0.0), axis=0, keepdims=True)
        n_sel = jnp.zeros_like(s1)
        for j in range(P_TOPK):
            n_sel = jnp.where(s1 + v2[j] >= c16, float(j + 1), n_sel)
        sl = slice(h * P_NKEYS, (h + 1) * P_NKEYS)
        n_ref[sl, :] = n_sel
        c_ref[sl, :] = jnp.exp(s1 - v1[0]) / z
        r2_ref[sl, :] = r2.astype(BF16)
        e2_ref[sl, :] = jnp.exp(s2 - v2[0]).astype(BF16)


def _peer_scores(h, g, wq_t, k1, k2):
    T = h.shape[0]
    tm = _pick_tile(T, 256, 128)
    HK = P_HEADS * P_NKEYS
    tok = lambda i: (0, i)
    f32_side = jax.ShapeDtypeStruct((HK, T), F32)
    bf16_side = jax.ShapeDtypeStruct((HK, T), BF16)
    return pl.pallas_call(
        _peer_score_kernel,
        grid=(T // tm,),
        in_specs=[pl.BlockSpec((tm, D_MODEL), lambda i: (i, 0)),
                  pl.BlockSpec((1, D_MODEL), lambda i: (0, 0)),
                  pl.BlockSpec((P_HEADS * P_DQ, D_MODEL), lambda i: (0, 0)),
                  pl.BlockSpec((P_HEADS, P_NKEYS, P_DQ // 2), lambda i: (0, 0, 0)),
                  pl.BlockSpec((P_HEADS, P_NKEYS, P_DQ // 2), lambda i: (0, 0, 0))],
        out_specs=[pl.BlockSpec((D_MODEL, tm), tok)] + [pl.BlockSpec((HK, tm), tok)] * 4,
        out_shape=[jax.ShapeDtypeStruct((D_MODEL, T), BF16), f32_side, f32_side, bf16_side, bf16_side],
        compiler_params=_params(("parallel",)),
        name="peer_scores",
    )(h, g, wq_t, k1, k2)


PEER_TE = 1024
PEER_NJ = P_NEXP // PEER_TE
PEER_DROWS = 512


def _peer_expert_kernel(xt_ref, n_ref, c_ref, r2_ref, e2_ref, u_ref, vt_ref, o_ref, w0_ref, w1_ref, a_ref):
    j = pl.program_id(1)
    nj = pl.num_programs(1) - 1
    na = PEER_TE // P_NKEYS
    w_ref = (w0_ref, w1_ref)

    def gate_block(slot, al):
        a = j * na + al
        gate = None
        for h in range(P_HEADS):
            n_sel = n_ref[pl.ds(h * P_NKEYS + a, 1), :].astype(BF16)
            coef = c_ref[pl.ds(h * P_NKEYS + a, 1), :].astype(BF16)
            sl = slice(h * P_NKEYS, (h + 1) * P_NKEYS)
            term = jnp.where(r2_ref[sl, :] < n_sel, e2_ref[sl, :], 0.0) * coef
            gate = term if gate is None else gate + term
        w_ref[slot][al * P_NKEYS:(al + 1) * P_NKEYS, :] = gate

    def activate_block(slot, al):
        rows = slice(al * P_NKEYS, (al + 1) * P_NKEYS)
        x = a_ref[rows, :]
        act = 0.5 * x * (1.0 + lax.erf(x * (2.0 ** -0.5)))
        w_ref[slot][rows, :] = w_ref[slot][rows, :] * act

    def accumulate(slot, k):
        rows = slice(k * PEER_DROWS, (k + 1) * PEER_DROWS)
        o_ref[rows, :] += jnp.dot(vt_ref[rows, :], w_ref[slot][...], preferred_element_type=F32)

    def stage_a(slot):
        for al in range(na):
            gate_block(slot, al)
        half = PEER_TE // 2
        for r in range(2):
            rows = slice(r * half, (r + 1) * half)
            a_ref[rows, :] = jnp.dot(u_ref[rows, :], xt_ref[...],
                                     preferred_element_type=F32).astype(BF16)
        for al in range(na):
            activate_block(slot, al)

    def stage_b(slot):
        for k in range(D_MODEL // PEER_DROWS):
            accumulate(slot, k)

    @pl.when(j == 0)
    def _():
        o_ref[...] = jnp.zeros_like(o_ref)
        stage_a(0)

    for parity in range(2):
        @pl.when((j > 0) & (j < nj) & (j % 2 == parity))
        def _():
            stage_a(parity)
            stage_b(1 - parity)

    @pl.when(j == nj)
    def _():
        stage_b((PEER_NJ - 1) % 2)


def _peer_experts(xt, n_sel, coef, r2, e2, u, vt):
    T = xt.shape[1]
    tm = _pick_tile(T, 768, 256)
    te = PEER_TE
    nj = PEER_NJ
    HK = P_HEADS * P_NKEYS
    once = pl.Buffered(1) if tm > 512 else None
    tok = lambda i, j: (0, i)
    return pl.pallas_call(
        _peer_expert_kernel,
        grid=(T // tm, nj + 1),
        in_specs=[pl.BlockSpec((D_MODEL, tm), tok, pipeline_mode=once)]
                 + [pl.BlockSpec((HK, tm), tok, pipeline_mode=once)] * 4
                 + [pl.BlockSpec((te, D_MODEL), lambda i, j: (jnp.minimum(j, nj - 1), 0)),
                    pl.BlockSpec((None, D_MODEL, te), lambda i, j: (jnp.maximum(j - 1, 0), 0, 0))],
        out_specs=pl.BlockSpec((D_MODEL, tm), tok),
        out_shape=jax.ShapeDtypeStruct((D_MODEL, T), F32),
        scratch_shapes=[pltpu.VMEM((te, tm), BF16), pltpu.VMEM((te, tm), BF16), pltpu.VMEM((te, tm), BF16)],
        compiler_params=_params(("parallel", "arbitrary")),
        name="peer_experts",
    )(xt, n_sel, coef, r2, e2, u, vt)


def _add_transposed_kernel(h_ref, yt_ref, o_ref):
    o_ref[...] = h_ref[...] + yt_ref[...].T


def _add_transposed(h, yt):
    T, D = h.shape
    tm = _pick_tile(T, 512, 128)
    return pl.pallas_call(
        _add_transposed_kernel,
        grid=(T // tm,),
        in_specs=[pl.BlockSpec((tm, D), lambda i: (i, 0)), pl.BlockSpec((D, tm), lambda i: (0, i))],
        out_specs=pl.BlockSpec((tm, D), lambda i: (i, 0)),
        out_shape=jax.ShapeDtypeStruct((T, D), F32),
        compiler_params=_params(("parallel",)),
        name="add_transposed",
    )(h, yt)


def _prep_layer(l, g_mix, w_in, b_mgate, g_mhead, g_q, g_k, attn_sink, w_branch_m, w_branch_a,
                b_merge, w_out, g_ffn, w_pq, sub_k1, sub_k2, u_tab, v_tab):
    n_gate0 = 2 * M_QK + 2 * M_V
    n_gm0 = n_gate0 + N_GATE + A_Q + 2 * A_KV
    w = w_in[l]
    w_main = jnp.concatenate([w[:, n_gm0:], w[:, :n_gate0], w[:, n_gate0 + N_GATE:n_gm0]], axis=1).astype(BF16)
    w_gate = jnp.pad(w[:, n_gate0:n_gate0 + N_GATE], ((0, 0), (0, LANES - N_GATE))).astype(BF16)
    bias = jnp.pad(b_mgate[l], (0, LANES - N_GATE)).reshape(1, LANES)
    return dict(
        g_mix=g_mix[l].reshape(1, D_MODEL), w_main=_column_tiles(w_main, 1024),
        w_gate=_column_tiles(w_gate, LANES), bias=bias,
        g_mhead=g_mhead[l].reshape(1, M_V), g_q=g_q[l].reshape(1, A_DH), g_k=g_k[l].reshape(1, A_DH),
        sink=attn_sink[l], wm=w_branch_m[l].astype(BF16), wa=w_branch_a[l].astype(BF16),
        b_merge=b_merge[l].reshape(1, 2 * D_MODEL), wo=w_out[l].astype(BF16),
        g_ffn=g_ffn[l].reshape(1, D_MODEL), wq_t=w_pq[l].T.astype(BF16),
        k1=sub_k1[l].astype(BF16), k2=sub_k2[l].astype(BF16),
        u=u_tab[l].astype(BF16),
        vt=v_tab[l].astype(BF16).reshape(PEER_NJ, PEER_TE, D_MODEL).transpose(0, 2, 1))


def _layer(h, p, cos, sin, B, nc):
    proj = _norm_proj(h, p["g_mix"], p["w_main"], BF16)
    gates = _norm_proj(h, p["g_mix"], p["w_gate"], F32)
    hf, hb = _mlstm(proj, gates, p["bias"], B, nc)
    q, k, v = _qk_prep(proj, cos, sin, p["g_q"], p["g_k"])
    oa = _attention(q, k, v, p["sink"], B, nc)
    z = _merge(hf, hb, proj, p["g_mhead"], oa, p["b_merge"], p["wm"], p["wa"])
    h = _out_proj(z, p["wo"], h)
    xt, n_sel, coef, r2, e2 = _peer_scores(h, p["g_ffn"], p["wq_t"], p["k1"], p["k2"])
    return _add_transposed(h, _peer_experts(xt, n_sel, coef, r2, e2, p["u"], p["vt"]))


def _trunk(x, meta_tokens, layers):
    B, L, D = x.shape
    Lp = BLOCK + L
    nc = Lp // BLOCK
    meta = jnp.broadcast_to(meta_tokens.astype(x.dtype)[None], (B, N_META, D))
    h = jnp.concatenate([jnp.zeros((B, META_START, D), x.dtype), meta, x], axis=1).reshape(B * Lp, D)
    pos = (jnp.arange(Lp) - META_START).astype(F32)
    half = A_DH // 2
    inv = ROPE_THETA ** (-jnp.arange(half, dtype=F32) / half)
    ang = pos[:, None] * inv[None, :]
    cos = jnp.tile(jnp.concatenate([jnp.cos(ang), jnp.cos(ang)], axis=1), (B, 1))
    sin = jnp.tile(jnp.concatenate([-jnp.sin(ang), jnp.sin(ang)], axis=1), (B, 1))
    for p in layers:
        h = _layer(h, p, cos, sin, B, nc)
    return h.reshape(B, Lp, D)[:, BLOCK:]


def kernel(x_prompt, x_sample, meta_tokens, g_mix, w_in, b_mgate, g_mhead, g_q, g_k, attn_sink,
           w_branch_m, w_branch_a, b_merge, w_out, g_ffn, w_pq, sub_k1, sub_k2, u_tab, v_tab):
    weights = (g_mix, w_in, b_mgate, g_mhead, g_q, g_k, attn_sink, w_branch_m, w_branch_a,
               b_merge, w_out, g_ffn, w_pq, sub_k1, sub_k2, u_tab, v_tab)
    layers = [_prep_layer(l, *weights) for l in range(w_in.shape[0])]
    return (_trunk(x_prompt, meta_tokens, layers), _trunk(x_sample, meta_tokens, layers))
```
